```python
import math
import jax, jax.numpy as jnp
from jax import lax
import numpy as np

D_MODEL = 1024
BATCH = 8
SEQ = 2048
DEPTH = 4
DEC_BATCH = 128
DEC_SEQ = 4
PAST_LEN = 16384
PAGE_SIZE = 128

N_EVEN = (DEPTH + 1) // 2
N_ODD = DEPTH // 2

RWKV_HEAD = 64
RWKV_W = D_MODEL // 2
RWKV_HEADS = RWKV_W // RWKV_HEAD
RWKV_W_LORA = 64
RWKV_A_LORA = 64
RWKV_PROJ = 3 * RWKV_W + RWKV_W_LORA + RWKV_A_LORA
RWKV_GN_EPS = 64e-5

GLA_VW = D_MODEL // 2
GLA_KW = GLA_VW // 2
GLA_HEADS = 4
GLA_DK = GLA_KW // GLA_HEADS
GLA_DV = GLA_VW // GLA_HEADS
GLA_LORA = 16
GLA_GATE_NORM = 16.0
GLA_CHUNK = 64
GLA_PROJ = 2 * GLA_KW + GLA_VW + GLA_LORA

EVEN_WIDTH = RWKV_W + GLA_VW
EVEN_PROJ = RWKV_PROJ + GLA_PROJ + EVEN_WIDTH

CONV_C = D_MODEL
CONV_W = 31
ODD_PROJ = 3 * CONV_C

RMS_EPS = 1e-6
LN_EPS = 1e-5

kernel_name = 'hybrid_rwkv7_gla_conformer_conv_step'


def rmsnorm(x, g):
    xf = x.astype(jnp.float32)
    y = xf * lax.rsqrt(jnp.mean(xf * xf, axis=-1, keepdims=True) + RMS_EPS) * g.astype(jnp.float32)
    return y.astype(x.dtype)


def layernorm(x, g, b):
    xf = x.astype(jnp.float32)
    mu = jnp.mean(xf, axis=-1, keepdims=True)
    var = jnp.mean(jnp.square(xf - mu), axis=-1, keepdims=True)
    y = (xf - mu) * lax.rsqrt(var + LN_EPS) * g.astype(jnp.float32) + b.astype(jnp.float32)
    return y.astype(x.dtype)


def _rwkv_step(S, inp):
    r, w, k, v, kk, a = inp
    sa = jnp.einsum('bhvk,bhk->bhv', S, -kk)
    S = S * w[:, :, None, :] + sa[..., None] * (kk * a)[:, :, None, :] + v[..., None] * k[:, :, None, :]
    return S, jnp.einsum('bhvk,bhk->bhv', S, r)


def rwkv7_mix(p, prev_row, s0, mu, w0, w_up, a0, a_up, k_k, k_a, r_k, ln_w, ln_b):
    B, L, _ = p.shape
    H, K = RWKV_HEADS, RWKV_HEAD
    f32 = jnp.float32
    p_prev = jnp.concatenate([prev_row[:, None, :].astype(p.dtype), p[:, :-1]], axis=1)
    xm = p + (p_prev - p) * mu
    o1, o2, o3, o4 = RWKV_W, 2 * RWKV_W, 3 * RWKV_W, 3 * RWKV_W + RWKV_W_LORA
    r, k, v, xw, xa = xm[..., :o1], xm[..., o1:o2], xm[..., o2:o3], xm[..., o3:o4], xm[..., o4:]
    w = -jax.nn.softplus(-(w0 + jnp.tanh(xw) @ w_up).astype(f32)) - 0.5
    decay = jnp.exp(-jnp.exp(w))
    a = jax.nn.sigmoid((a0 + xa @ a_up).astype(f32))
    heads = lambda t: t.astype(f32).reshape(B, L, H, K)
    r, k, v, decay, a = heads(r), heads(k), heads(v), heads(decay), heads(a)
    kk = k * k_k.astype(f32).reshape(H, K)
    kk = kk / jnp.maximum(jnp.sqrt(jnp.sum(kk * kk, axis=-1, keepdims=True)), 1e-12)
    k = k * (1.0 + (a - 1.0) * k_a.astype(f32).reshape(H, K))
    seq = lambda t: jnp.moveaxis(t, 1, 0)
    s_final, y = lax.scan(_rwkv_step, s0.astype(f32),
                          (seq(r), seq(decay), seq(k), seq(v), seq(kk), seq(a)))
    y = jnp.moveaxis(y, 0, 1)
    mean = jnp.mean(y, axis=-1, keepdims=True)
    var = jnp.mean(jnp.square(y - mean), axis=-1, keepdims=True)
    y = ((y - mean) * lax.rsqrt(var + RWKV_GN_EPS)).reshape(B, L, RWKV_W)
    y = y * ln_w.astype(f32) + ln_b.astype(f32)
    bonus = jnp.sum(r * k * r_k.astype(f32), axis=-1, keepdims=True) * v
    out = y + bonus.reshape(B, L, RWKV_W)
    return out.astype(p.dtype), p[:, -1], s_final.astype(s0.dtype)


def _gla_chunk_step(S, inp):
    d, kv = inp
    return d[..., None] * S + kv, S


def gla_mix(p, s0, g_up, g_b, g_norm):
    B, L, _ = p.shape
    H, DK, DV = GLA_HEADS, GLA_DK, GLA_DV
    f32 = jnp.float32
    q = p[..., :GLA_KW]
    k = p[..., GLA_KW:2 * GLA_KW]
    v = p[..., 2 * GLA_KW:2 * GLA_KW + GLA_VW]
    gd = p[..., 2 * GLA_KW + GLA_VW:]
    la = jax.nn.log_sigmoid((gd @ g_up + g_b).astype(f32)) / GLA_GATE_NORM
    C = math.gcd(L, GLA_CHUNK)
    NC = L // C
    chunks = lambda t, d: t.astype(f32).reshape(B, NC, C, H, d).transpose(0, 3, 1, 2, 4)
    q = chunks(q, DK) * (DK ** -0.5)
    k = chunks(k, DK)
    v = chunks(v, DV)
    b = jnp.cumsum(chunks(la, DK), axis=3)
    b_last = b[:, :, :, -1:, :]
    qg = q * jnp.exp(b)
    kg = k * jnp.exp(-b)
    mask = jnp.tril(jnp.ones((C, C), dtype=bool))
    att = jnp.where(mask, jnp.einsum('bhncd,bhnsd->bhncs', qg, kg), 0.0)
    o = jnp.einsum('bhncs,bhnsv->bhncv', att, v)
    kv = jnp.einsum('bhncd,bhncv->bhndv', k * jnp.exp(b_last - b), v)
    dec = jnp.exp(b_last[:, :, :, 0, :])
    s_final, s_before = lax.scan(_gla_chunk_step, s0.astype(f32),
                                 (jnp.moveaxis(dec, 2, 0), jnp.moveaxis(kv, 2, 0)))
    s_before = jnp.moveaxis(s_before, 0, 2)
    o = o + jnp.einsum('bhncd,bhndv->bhncv', qg, s_before)
    o = o * lax.rsqrt(jnp.mean(o * o, axis=-1, keepdims=True) + RMS_EPS) * g_norm.astype(f32)
    o = o.transpose(0, 2, 3, 1, 4).reshape(B, L, GLA_VW)
    return o.astype(p.dtype), s_final.astype(s0.dtype)


def even_layer(x, shift0, s_rwkv0, s_gla0, norm, w_in, mu, w0, w_up, a0, a_up, k_k, k_a, r_k,
               ln_w, ln_b, g_up, g_b, g_norm, w_out):
    h = rmsnorm(x, norm)
    p = h @ w_in
    p_rwkv = p[..., :RWKV_PROJ]
    p_gla = p[..., RWKV_PROJ:RWKV_PROJ + GLA_PROJ]
    gate = p[..., RWKV_PROJ + GLA_PROJ:]
    o_r, shift_new, s_r = rwkv7_mix(p_rwkv, shift0, s_rwkv0, mu, w0, w_up, a0, a_up, k_k, k_a, r_k, ln_w, ln_b)
    o_g, s_g = gla_mix(p_gla, s_gla0, g_up, g_b, g_norm)
    o = jnp.concatenate([o_r, o_g], axis=-1) * jax.nn.silu(gate)
    return x + o @ w_out, shift_new, s_r, s_g


def odd_layer(x, buf, norm, w_in, b_in, conv_w, conv_b, ln_w, ln_b, w_out):
    h = rmsnorm(x, norm)
    p = h @ w_in + b_in
    u = p[..., :CONV_C] * jax.nn.sigmoid(p[..., CONV_C:2 * CONV_C])
    g = p[..., 2 * CONV_C:]
    ext = jnp.concatenate([buf.astype(u.dtype), u], axis=1)
    y = lax.conv_general_dilated(ext, conv_w[:, None, :].astype(u.dtype), (1,), 'VALID',
                                 dimension_numbers=('NWC', 'WIO', 'NWC'),
                                 feature_group_count=CONV_C) + conv_b
    y = jax.nn.silu(layernorm(y, ln_w, ln_b)) * jax.nn.silu(g)
    return x + y @ w_out, ext[:, -(CONV_W - 1):]


def setup_inputs(seed: int = 0) -> dict:
    key = jax.random.key(seed)
    ks = iter(jax.random.split(key, 40))
    nrm = lambda shape, s: s * jax.random.normal(next(ks), shape, jnp.float32)
    E, O = N_EVEN, N_ODD
    return {
        'x_prompt': nrm((BATCH, SEQ, D_MODEL), 1.0),
        'x_sample': nrm((DEC_BATCH, DEC_SEQ, D_MODEL), 1.0),
        'state_rwkv_shift': nrm((E, DEC_BATCH, RWKV_PROJ), 1.0),
        'state_rwkv': nrm((E, DEC_BATCH, RWKV_HEADS, RWKV_HEAD, RWKV_HEAD), 0.3),
        'state_gla': nrm((E, DEC_BATCH, GLA_HEADS, GLA_DK, GLA_DV), 0.3),
        'cache_conv': nrm((O, DEC_BATCH, CONV_W - 1, CONV_C), 0.5),
        'norm_even': 1.0 + nrm((E, D_MODEL), 0.05),
        'w_in_even': nrm((E, D_MODEL, EVEN_PROJ), D_MODEL ** -0.5),
        'rwkv_mu': jax.random.uniform(next(ks), (E, RWKV_PROJ), jnp.float32),
        'rwkv_w0': nrm((E, RWKV_W), 0.5),
        'rwkv_w_up': nrm((E, RWKV_W_LORA, RWKV_W), RWKV_W_LORA ** -0.5),
        'rwkv_a0': nrm((E, RWKV_W), 0.1),
        'rwkv_a_up': nrm((E, RWKV_A_LORA, RWKV_W), RWKV_A_LORA ** -0.5),
        'rwkv_k_k': 0.85 + nrm((E, RWKV_W), 0.05),
        'rwkv_k_a': 1.0 + nrm((E, RWKV_W), 0.05),
        'rwkv_r_k': nrm((E, RWKV_HEADS, RWKV_HEAD), 0.1),
        'rwkv_ln_w': 1.0 + nrm((E, RWKV_W), 0.05),
        'rwkv_ln_b': nrm((E, RWKV_W), 0.02),
        'gla_g_up': nrm((E, GLA_LORA, GLA_KW), GLA_LORA ** -0.5),
        'gla_g_b': nrm((E, GLA_KW), 0.1),
        'gla_norm': 1.0 + nrm((E, GLA_DV), 0.05),
        'w_out_even': nrm((E, EVEN_WIDTH, D_MODEL), EVEN_WIDTH ** -0.5),
        'norm_odd': 1.0 + nrm((O, D_MODEL), 0.05),
        'w_in_odd': nrm((O, D_MODEL, ODD_PROJ), D_MODEL ** -0.5),
        'b_in_odd': nrm((O, ODD_PROJ), 0.02),
        'conv_w': nrm((O, CONV_W, CONV_C), CONV_W ** -0.5),
        'conv_b': nrm((O, CONV_C), 0.02),
        'conv_ln_w': 1.0 + nrm((O, CONV_C), 0.05),
        'conv_ln_b': nrm((O, CONV_C), 0.02),
        'w_out_odd': nrm((O, CONV_C, D_MODEL), CONV_C ** -0.5),
        'final_norm': 1.0 + nrm((D_MODEL,), 0.05),
    }


def reference(x_prompt, x_sample, state_rwkv_shift, state_rwkv, state_gla, cache_conv,
              norm_even, w_in_even, rwkv_mu, rwkv_w0, rwkv_w_up, rwkv_a0, rwkv_a_up, rwkv_k_k,
              rwkv_k_a, rwkv_r_k, rwkv_ln_w, rwkv_ln_b, gla_g_up, gla_g_b, gla_norm, w_out_even,
              norm_odd, w_in_odd, b_in_odd, conv_w, conv_b, conv_ln_w, conv_ln_b, w_out_odd,
              final_norm):
    dt = x_prompt.dtype
    bp = x_prompt.shape[0]
    shift_p0 = jnp.zeros((bp, RWKV_PROJ), dt)
    rwkv_p0 = jnp.zeros((bp, RWKV_HEADS, RWKV_HEAD, RWKV_HEAD), dt)
    gla_p0 = jnp.zeros((bp, GLA_HEADS, GLA_DK, GLA_DV), dt)
    conv_p0 = jnp.zeros((bp, CONV_W - 1, CONV_C), dt)
    xp, xs = x_prompt, x_sample
    sh_p, sh_s, sr_p, sr_s, sg_p, sg_s, cb_p, cb_s = [], [], [], [], [], [], [], []
    for layer in range(DEPTH):
        i = layer // 2
        if layer % 2 == 0:
            ew = (norm_even[i], w_in_even[i], rwkv_mu[i], rwkv_w0[i], rwkv_w_up[i], rwkv_a0[i],
                  rwkv_a_up[i], rwkv_k_k[i], rwkv_k_a[i], rwkv_r_k[i], rwkv_ln_w[i], rwkv_ln_b[i],
                  gla_g_up[i], gla_g_b[i], gla_norm[i], w_out_even[i])
            xp, a1, a2, a3 = even_layer(xp, shift_p0, rwkv_p0, gla_p0, *ew)
            xs, b1, b2, b3 = even_layer(xs, state_rwkv_shift[i], state_rwkv[i], state_gla[i], *ew)
            sh_p.append(a1); sr_p.append(a2); sg_p.append(a3)
            sh_s.append(b1); sr_s.append(b2); sg_s.append(b3)
        else:
            ow = (norm_odd[i], w_in_odd[i], b_in_odd[i], conv_w[i], conv_b[i], conv_ln_w[i],
                  conv_ln_b[i], w_out_odd[i])
            xp, c1 = odd_layer(xp, conv_p0, *ow)
            xs, c2 = odd_layer(xs, cache_conv[i], *ow)
            cb_p.append(c1); cb_s.append(c2)
    y_prompt = rmsnorm(xp, final_norm)
    y_sample = rmsnorm(xs, final_norm)
    return (y_prompt, y_sample,
            jnp.stack(sh_p), jnp.stack(sh_s),
            jnp.stack(sr_p), jnp.stack(sr_s),
            jnp.stack(sg_p), jnp.stack(sg_s),
            jnp.stack(cb_p), jnp.stack(cb_s))
```

```python
import functools
import math

import jax
import jax.numpy as jnp
from jax import lax
from jax.experimental import pallas as pl
from jax.experimental.pallas import tpu as pltpu

f32 = jnp.float32
bf16 = jnp.bfloat16

D_MODEL = 1024
RWKV_HEAD = 64
RWKV_HEADS = 8
RWKV_W = 512
RWKV_LORA = 64
RWKV_PROJ = 3 * RWKV_W + 2 * RWKV_LORA
GLA_HEADS = 4
GLA_DK = 64
GLA_DV = 128
GLA_KW = GLA_HEADS * GLA_DK
GLA_VW = GLA_HEADS * GLA_DV
GLA_LORA = 16
GLA_LORA_PAD = 128
GLA_PROJ = 2 * GLA_KW + GLA_VW + GLA_LORA
EVEN_WIDTH = RWKV_W + GLA_VW
Q_OFF = RWKV_PROJ
K_OFF = Q_OFF + GLA_KW
V_OFF = K_OFF + GLA_KW
GD_OFF = V_OFF + GLA_VW
GATE_OFF = GD_OFF + GLA_LORA_PAD
EVEN_PROJ_PAD = GATE_OFF + EVEN_WIDTH
CONV_C = 1024
CONV_W = 31
CONV_HIST = CONV_W - 1
CONV_HIST_PAD = 32
ODD_PROJ = 3 * CONV_C

RMS_EPS = 1e-6
LN_EPS = 1e-5
RWKV_GN_EPS = 64e-5
GLA_GATE_NORM = 16.0
PROMPT_CHUNK = 64
SAMPLE_PAD_LEN = 8
CONV_BLOCK = 128
CONV_ROW_TILE = 32
LANE = 128
VMEM_LIMIT_BYTES = 48 * 1024 * 1024

NN = ((1,), (0,))
NT = ((1,), (1,))
TN = ((0,), (0,))


def _dg(a, b, dims=NN):
    return lax.dot_general(a, b, (dims, ((), ())), preferred_element_type=f32)


def _dot1(a, b, dims=NN):
    return _dg(a.astype(bf16), b.astype(bf16), dims)


def _split2(x):
    hi = x.astype(bf16)
    lo = (x - hi.astype(f32)).astype(bf16)
    return hi, lo


def _split3(x):
    hi = x.astype(bf16)
    r1 = x - hi.astype(f32)
    mid = r1.astype(bf16)
    lo = (r1 - mid.astype(f32)).astype(bf16)
    return hi, mid, lo


def _dot3(a, b):
    ah, al = _split2(a)
    bh, bl = _split2(b)
    return _dg(ah, bh) + _dg(ah, bl) + _dg(al, bh)


def _dot_exact_lhs(m, x):
    hi, mid, lo = _split3(x)
    return _dg(m, hi) + _dg(m, mid) + _dg(m, lo)


def _dot_exact_rhs(x, m):
    hi, mid, lo = _split3(x)
    return _dg(hi, m) + _dg(mid, m) + _dg(lo, m)


def _sigmoid(x):
    return 1.0 / (1.0 + jnp.exp(-x))


def _log_sigmoid(x):
    return jnp.minimum(x, 0.0) - jnp.log(1.0 + jnp.exp(-jnp.abs(x)))


def _rmsnorm(x, g):
    return x * lax.rsqrt(jnp.mean(x * x, axis=-1, keepdims=True) + RMS_EPS) * g


def _inproj_kernel(x_ref, g_ref, w_ref, b_ref, o_ref):
    h = _rmsnorm(x_ref[...], g_ref[...])
    o_ref[...] = _dg(h.astype(bf16), w_ref[...]) + b_ref[...]


def _inproj(x2d, g, w, b):
    m, n = x2d.shape[0], w.shape[1]
    tm = 256
    return pl.pallas_call(
        _inproj_kernel,
        grid=(m // tm,),
        in_specs=[
            pl.BlockSpec((tm, D_MODEL), lambda i: (i, 0)),
            pl.BlockSpec((1, D_MODEL), lambda i: (0, 0)),
            pl.BlockSpec((D_MODEL, n), lambda i: (0, 0)),
            pl.BlockSpec((1, n), lambda i: (0, 0)),
        ],
        out_specs=pl.BlockSpec((tm, n), lambda i: (i, 0)),
        out_shape=jax.ShapeDtypeStruct((m, n), f32),
        compiler_params=pltpu.CompilerParams(
            dimension_semantics=("arbitrary",), vmem_limit_bytes=VMEM_LIMIT_BYTES),
        name="inproj",
    )(x2d, g, w, b)


def _outproj_kernel(o_ref, x_ref, w_ref, fn_ref, y_ref, *, final):
    y = x_ref[...] + _dg(o_ref[...].astype(bf16), w_ref[...])
    if final:
        y = _rmsnorm(y, fn_ref[...])
    y_ref[...] = y


def _outproj(o2d, x2d, w, fn, final):
    m = x2d.shape[0]
    tm = 512
    return pl.pallas_call(
        functools.partial(_outproj_kernel, final=final),
        grid=(m // tm,),
        in_specs=[
            pl.BlockSpec((tm, D_MODEL), lambda i: (i, 0)),
            pl.BlockSpec((tm, D_MODEL), lambda i: (i, 0)),
            pl.BlockSpec((D_MODEL, D_MODEL), lambda i: (0, 0)),
            pl.BlockSpec((1, D_MODEL), lambda i: (0, 0)),
        ],
        out_specs=pl.BlockSpec((tm, D_MODEL), lambda i: (i, 0)),
        out_shape=jax.ShapeDtypeStruct((m, D_MODEL), f32),
        compiler_params=pltpu.CompilerParams(
            dimension_semantics=("arbitrary",), vmem_limit_bytes=VMEM_LIMIT_BYTES),
        name="outproj",
    )(o2d, x2d, w, fn)


def _tri_inverse(a, t, row, col):
    minv = jnp.where(row == col, 1.0, 0.0).astype(f32)
    blk = 1
    while blk < t:
        lower_left = ((row // (2 * blk)) == (col // (2 * blk))) & ((row // blk) % 2 == 1) & ((col // blk) % 2 == 0)
        al = jnp.where(lower_left, a, 0.0)
        minv = minv + _dot1(_dot1(minv, al), minv)
        blk *= 2
    return minv


def _even_mix_kernel(p_ref, shift_ref, sr_in_ref, sg_in_ref, mu_ref, w0_ref, wup_ref, a0_ref, aup_ref,
                     kk_ref, ka_ref, rk_ref, lnw_ref, lnb_ref, gup_ref, gb_ref, gn_ref, bd_ref,
                     og_ref, sr_ref, sg_ref, prev_ref, *, t, n_valid):
    c = pl.program_id(1)

    @pl.when(c == 0)
    def _():
        sr_ref[...] = sr_in_ref[...]
        sg_ref[...] = sg_in_ref[...]
        prev_ref[...] = shift_ref[0]

    row = lax.broadcasted_iota(jnp.int32, (t, t), 0)
    col = lax.broadcasted_iota(jnp.int32, (t, t), 1)
    strict = row > col
    incl = row >= col
    tril = jnp.where(incl, 1.0, 0.0).astype(bf16)
    rowv = lax.broadcasted_iota(jnp.int32, (t, 1), 0)
    valid = None if n_valid == t else rowv < n_valid
    bd = bd_ref[...]

    def head_sum(x):
        return _dot_exact_rhs(x, bd)

    pr = p_ref[0, :, 0:RWKV_PROJ]
    p_prev = jnp.where(rowv == 0, prev_ref[...], pltpu.roll(pr, 1, 0))
    prev_ref[...] = pr[t - 1:t, :]
    xm = pr + (p_prev - pr) * mu_ref[...]
    r = xm[:, 0:RWKV_W]
    k = xm[:, RWKV_W:2 * RWKV_W]
    v = xm[:, 2 * RWKV_W:3 * RWKV_W]
    xwa = xm[:, 3 * RWKV_W:RWKV_PROJ]
    wl = w0_ref[...] + _dot3(jnp.tanh(xwa), wup_ref[...])
    al = a0_ref[...] + _dot3(xwa, aup_ref[...])
    ld = (-math.exp(-0.5)) * _sigmoid(wl)
    a = _sigmoid(al)
    kkr = k * kk_ref[...]
    kkn = kkr / jnp.maximum(jnp.sqrt(head_sum(kkr * kkr)), 1e-12)
    k2 = k * (1.0 + (a - 1.0) * ka_ref[...])
    if valid is not None:
        ld = jnp.where(valid, ld, 0.0)
        kkn = jnp.where(valid, kkn, 0.0)
        k2 = jnp.where(valid, k2, 0.0)
    c_in = _dot_exact_lhs(tril, ld)
    c_ex = c_in - ld
    g_in = jnp.exp(c_in)
    ginv = jnp.exp(-c_in)
    abar = -kkn * jnp.exp(c_ex)
    bbar = kkn * a * ginv
    kbar = k2 * ginv
    rbar = r * g_in

    ys = []
    for h in range(RWKV_HEADS):
        sl = slice(h * RWKV_HEAD, (h + 1) * RWKV_HEAD)
        s = sr_ref[0, h]
        lm = jnp.concatenate([abar[:, sl], rbar[:, sl]], axis=0).astype(bf16)
        rm = jnp.concatenate([bbar[:, sl], kbar[:, sl]], axis=0).astype(bf16)
        vh = v[:, sl]
        g = _dg(lm, rm, NT)
        a_ab = jnp.where(strict, g[:t, :t], 0.0)
        a_ak = jnp.where(strict, g[:t, t:], 0.0)
        a_rb = jnp.where(incl, g[t:, :t], 0.0)
        a_rk = jnp.where(incl, g[t:, t:], 0.0)
        minv = _tri_inverse(a_ab, t, row, col)
        ls = _dg(lm, s.astype(bf16), NT)
        z = _dot1(minv, ls[:t] + _dot1(a_ak, vh))
        zv = jnp.concatenate([z, vh], axis=0).astype(bf16)
        ys.append(ls[t:] + _dg(jnp.concatenate([a_rb, a_rk], axis=1).astype(bf16), zv))
        sr_ref[0, h] = (s + _dg(zv, rm, TN)) * g_in[t - 1:t, sl]

    y = jnp.concatenate(ys, axis=1)
    inv_k = 1.0 / RWKV_HEAD
    mean = head_sum(y) * inv_k
    dlt = y - mean
    var = head_sum(dlt * dlt) * inv_k
    yn = dlt * lax.rsqrt(var + RWKV_GN_EPS) * lnw_ref[...] + lnb_ref[...]
    o_r = yn + head_sum(r * k2 * rk_ref[...]) * v

    q = p_ref[0, :, Q_OFF:K_OFF]
    kg = p_ref[0, :, K_OFF:V_OFF]
    vg = p_ref[0, :, V_OFF:GD_OFF]
    gd = p_ref[0, :, GD_OFF:GATE_OFF]
    la = _log_sigmoid(_dot3(gd, gup_ref[...]) + gb_ref[...]) * (1.0 / GLA_GATE_NORM)
    if valid is not None:
        la = jnp.where(valid, la, 0.0)
        kg = jnp.where(valid, kg, 0.0)
    b = _dot_exact_lhs(tril, la)
    b_last = b[t - 1:t, :]
    qg = q * jnp.exp(b) * (GLA_DK ** -0.5)
    kgi = kg * jnp.exp(-b)
    kd = kg * jnp.exp(b_last - b)
    eb = jnp.exp(b_last)
    rk64 = lax.broadcasted_iota(jnp.int32, (GLA_DK, GLA_DK), 0)
    ck64 = lax.broadcasted_iota(jnp.int32, (GLA_DK, GLA_DK), 1)
    og = []
    for h in range(GLA_HEADS):
        sl = slice(h * GLA_DK, (h + 1) * GLA_DK)
        vh = vg[:, h * GLA_DV:(h + 1) * GLA_DV].astype(bf16)
        s = sg_ref[0, h]
        qh = qg[:, sl].astype(bf16)
        att = jnp.where(incl, _dg(qh, kgi[:, sl].astype(bf16), NT), 0.0)
        o = _dg(att.astype(bf16), vh) + _dg(qh, s.astype(bf16))
        e_col = jnp.sum(jnp.where(rk64 == ck64, eb[:, sl], 0.0), axis=1, keepdims=True)
        sg_ref[0, h] = e_col * s + _dg(kd[:, sl].astype(bf16), vh, TN)
        og.append(_rmsnorm(o, gn_ref[...]))
    o_g = jnp.concatenate(og, axis=1)

    gate = p_ref[0, :, GATE_OFF:EVEN_PROJ_PAD]
    og_ref[0] = jnp.concatenate([o_r, o_g], axis=1) * (gate * _sigmoid(gate))


def _even_mix(p, shift0, sr0, sg0, wts, t, n_valid):
    bsz, lp, _ = p.shape
    nc = lp // t
    const = lambda shape: pl.BlockSpec(shape, lambda b, c: (0,) * len(shape))
    return pl.pallas_call(
        functools.partial(_even_mix_kernel, t=t, n_valid=n_valid),
        grid=(bsz, nc),
        in_specs=[
            pl.BlockSpec((1, t, EVEN_PROJ_PAD), lambda b, c: (b, c, 0)),
            pl.BlockSpec((1, 1, RWKV_PROJ), lambda b, c: (b, 0, 0)),
            pl.BlockSpec((1, RWKV_HEADS, RWKV_HEAD, RWKV_HEAD), lambda b, c: (b, 0, 0, 0)),
            pl.BlockSpec((1, GLA_HEADS, GLA_DK, GLA_DV), lambda b, c: (b, 0, 0, 0)),
            const((1, RWKV_PROJ)),
            const((1, RWKV_W)), const((2 * RWKV_LORA, RWKV_W)),
            const((1, RWKV_W)), const((2 * RWKV_LORA, RWKV_W)),
            const((1, RWKV_W)), const((1, RWKV_W)), const((1, RWKV_W)),
            const((1, RWKV_W)), const((1, RWKV_W)),
            const((GLA_LORA_PAD, GLA_KW)), const((1, GLA_KW)), const((1, GLA_DV)),
            const((RWKV_W, RWKV_W)),
        ],
        out_specs=[
            pl.BlockSpec((1, t, EVEN_WIDTH), lambda b, c: (b, c, 0)),
            pl.BlockSpec((1, RWKV_HEADS, RWKV_HEAD, RWKV_HEAD), lambda b, c: (b, 0, 0, 0)),
            pl.BlockSpec((1, GLA_HEADS, GLA_DK, GLA_DV), lambda b, c: (b, 0, 0, 0)),
        ],
        out_shape=[
            jax.ShapeDtypeStruct((bsz, lp, EVEN_WIDTH), f32),
            jax.ShapeDtypeStruct((bsz, RWKV_HEADS, RWKV_HEAD, RWKV_HEAD), f32),
            jax.ShapeDtypeStruct((bsz, GLA_HEADS, GLA_DK, GLA_DV), f32),
        ],
        scratch_shapes=[pltpu.VMEM((1, RWKV_PROJ), f32)],
        compiler_params=pltpu.CompilerParams(
            dimension_semantics=("arbitrary", "arbitrary"), vmem_limit_bytes=VMEM_LIMIT_BYTES),
        name="even_mix",
    )(p, shift0, sr0, sg0, *wts)


def _odd_mix_kernel(p_ref, cache_ref, cw_ref, cb_ref, lnw_ref, lnb_ref,
                    o_ref, cache_out_ref, ext_ref, y_ref, *, tb, n_valid):
    c = pl.program_id(1)
    pad = CONV_HIST_PAD - CONV_HIST

    @pl.when(c == 0)
    def _():
        ext_ref[0:8, :] = jnp.zeros((8, CONV_C), f32)
        ext_ref[pad:CONV_HIST_PAD, :] = cache_ref[0]

    ext_ref[CONV_HIST_PAD:CONV_HIST_PAD + tb, :] = (
        p_ref[0, :, 0:CONV_C] * _sigmoid(p_ref[0, :, CONV_C:2 * CONV_C]))

    row_tile = min(CONV_ROW_TILE, tb)
    for rt in range(tb // row_tile):
        r0 = rt * row_tile
        for ct in range(CONV_C // LANE):
            ls = slice(ct * LANE, (ct + 1) * LANE)
            acc = jnp.broadcast_to(cb_ref[:, ls], (row_tile, LANE))
            for j in range(CONV_W):
                acc = acc + cw_ref[j:j + 1, ls] * ext_ref[pad + r0 + j:pad + r0 + j + row_tile, ls]
            y_ref[r0:r0 + row_tile, ls] = acc

    y = y_ref[...]
    mu = jnp.mean(y, axis=-1, keepdims=True)
    d = y - mu
    var = jnp.mean(d * d, axis=-1, keepdims=True)
    yn = d * lax.rsqrt(var + LN_EPS) * lnw_ref[...] + lnb_ref[...]
    g = p_ref[0, :, 2 * CONV_C:3 * CONV_C]
    o_ref[0] = (yn * _sigmoid(yn)) * (g * _sigmoid(g))

    @pl.when(c == pl.num_programs(1) - 1)
    def _():
        cache_out_ref[0] = ext_ref[pad + n_valid:pad + n_valid + CONV_HIST, :]

    ext_ref[0:CONV_HIST_PAD, :] = ext_ref[tb:tb + CONV_HIST_PAD, :]


def _odd_mix(p, cache, cw, cb, lnw, lnb, tb, n_valid):
    bsz, lp, _ = p.shape
    nc = lp // tb
    const = lambda shape: pl.BlockSpec(shape, lambda b, c: (0,) * len(shape))
    return pl.pallas_call(
        functools.partial(_odd_mix_kernel, tb=tb, n_valid=n_valid),
        grid=(bsz, nc),
        in_specs=[
            pl.BlockSpec((1, tb, ODD_PROJ), lambda b, c: (b, c, 0)),
            pl.BlockSpec((1, CONV_HIST, CONV_C), lambda b, c: (b, 0, 0)),
            const((CONV_W, CONV_C)), const((1, CONV_C)), const((1, CONV_C)), const((1, CONV_C)),
        ],
        out_specs=[
            pl.BlockSpec((1, tb, CONV_C), lambda b, c: (b, c, 0)),
            pl.BlockSpec((1, CONV_HIST, CONV_C), lambda b, c: (b, 0, 0)),
        ],
        out_shape=[
            jax.ShapeDtypeStruct((bsz, lp, CONV_C), f32),
            jax.ShapeDtypeStruct((bsz, CONV_HIST, CONV_C), f32),
        ],
        scratch_shapes=[pltpu.VMEM((CONV_HIST_PAD + tb + 8, CONV_C), f32), pltpu.VMEM((tb, CONV_C), f32)],
        compiler_params=pltpu.CompilerParams(
            dimension_semantics=("arbitrary", "arbitrary"), vmem_limit_bytes=VMEM_LIMIT_BYTES),
        name="odd_mix",
    )(p, cache, cw, cb, lnw, lnb)


def _row(x):
    return x.reshape(1, -1)


def kernel(x_prompt, x_sample, state_rwkv_shift, state_rwkv, state_gla, cache_conv, norm_even, w_in_even, rwkv_mu, rwkv_w0, rwkv_w_up, rwkv_a0, rwkv_a_up, rwkv_k_k, rwkv_k_a, rwkv_r_k, rwkv_ln_w, rwkv_ln_b, gla_g_up, gla_g_b, gla_norm, w_out_even, norm_odd, w_in_odd, b_in_odd, conv_w, conv_b, conv_ln_w, conv_ln_b, w_out_odd, final_norm):
    bp, lp, _ = x_prompt.shape
    bs, ls, _ = x_sample.shape
    depth = norm_even.shape[0] + norm_odd.shape[0]
    xs_pad = jnp.pad(x_sample, ((0, 0), (0, SAMPLE_PAD_LEN - ls), (0, 0)))
    groups = [
        dict(x=x_prompt.reshape(bp * lp, D_MODEL), b=bp, l=lp, nv=lp, t=PROMPT_CHUNK, tb=CONV_BLOCK, prompt=True),
        dict(x=xs_pad.reshape(bs * SAMPLE_PAD_LEN, D_MODEL), b=bs, l=SAMPLE_PAD_LEN, nv=ls, t=SAMPLE_PAD_LEN,
             tb=SAMPLE_PAD_LEN, prompt=False),
    ]
    bd = (jnp.arange(RWKV_W)[:, None] // RWKV_HEAD == jnp.arange(RWKV_W)[None, :] // RWKV_HEAD).astype(bf16)
    zeros_lora = jnp.zeros((RWKV_LORA, RWKV_W), f32)
    outs = [dict(shift=[], sr=[], sg=[], conv=[]) for _ in groups]

    for layer in range(depth):
        i = layer // 2
        final = layer == depth - 1
        if layer % 2 == 0:
            w = w_in_even[i]
            w_pad = jnp.concatenate(
                [w[:, :RWKV_PROJ + GLA_PROJ], jnp.zeros((D_MODEL, GLA_LORA_PAD - GLA_LORA), f32),
                 w[:, RWKV_PROJ + GLA_PROJ:]], axis=1).astype(bf16)
            b_pad = jnp.zeros((1, EVEN_PROJ_PAD), f32)
            wts = (
                _row(rwkv_mu[i]), _row(rwkv_w0[i]), jnp.concatenate([rwkv_w_up[i], zeros_lora], axis=0),
                _row(rwkv_a0[i]), jnp.concatenate([zeros_lora, rwkv_a_up[i]], axis=0),
                _row(rwkv_k_k[i]), _row(rwkv_k_a[i]), _row(rwkv_r_k[i]), _row(rwkv_ln_w[i]), _row(rwkv_ln_b[i]),
                jnp.concatenate([gla_g_up[i], jnp.zeros((GLA_LORA_PAD - GLA_LORA, GLA_KW), f32)], axis=0),
                _row(gla_g_b[i]), _row(gla_norm[i]), bd,
            )
            w_out = w_out_even[i].astype(bf16)
            for gi, g in enumerate(groups):
                p = _inproj(g["x"], _row(norm_even[i]), w_pad, b_pad).reshape(g["b"], g["l"], EVEN_PROJ_PAD)
                if g["prompt"]:
                    shift0 = jnp.zeros((g["b"], 1, RWKV_PROJ), f32)
                    sr0 = jnp.zeros((g["b"], RWKV_HEADS, RWKV_HEAD, RWKV_HEAD), f32)
                    sg0 = jnp.zeros((g["b"], GLA_HEADS, GLA_DK, GLA_DV), f32)
                else:
                    shift0 = state_rwkv_shift[i][:, None, :]
                    sr0 = state_rwkv[i]
                    sg0 = state_gla[i]
                og, sr, sg = _even_mix(p, shift0, sr0, sg0, wts, g["t"], min(g["nv"], g["t"]))
                outs[gi]["shift"].append(p[:, g["nv"] - 1, :RWKV_PROJ])
                outs[gi]["sr"].append(sr)
                outs[gi]["sg"].append(sg)
                g["x"] = _outproj(og.reshape(-1, EVEN_WIDTH), g["x"], w_out, _row(final_norm), final)
        else:
            w = w_in_odd[i].astype(bf16)
            w_out = w_out_odd[i].astype(bf16)
            for gi, g in enumerate(groups):
                p = _inproj(g["x"], _row(norm_odd[i]), w, _row(b_in_odd[i])).reshape(g["b"], g["l"], ODD_PROJ)
                cache = jnp.zeros((g["b"], CONV_HIST, CONV_C), f32) if g["prompt"] else cache_conv[i]
                y, cache_new = _odd_mix(p, cache, conv_w[i], _row(conv_b[i]), _row(conv_ln_w[i]),
                                        _row(conv_ln_b[i]), g["tb"], min(g["nv"], g["tb"]))
                outs[gi]["conv"].append(cache_new)
                g["x"] = _outproj(y.reshape(-1, CONV_C), g["x"], w_out, _row(final_norm), final)

    y_prompt = groups[0]["x"].reshape(bp, lp, D_MODEL)
    y_sample = groups[1]["x"].reshape(bs, SAMPLE_PAD_LEN, D_MODEL)[:, :ls]
    op, os_ = outs
    return (y_prompt, y_sample,
            jnp.stack(op["shift"]), jnp.stack(os_["shift"]),
            jnp.stack(op["sr"]), jnp.stack(os_["sr"]),
            jnp.stack(op["sg"]), jnp.stack(os_["sg"]),
            jnp.stack(op["conv"]), jnp.stack(os_["conv"]))
```

```python
import functools
import math

import jax
import jax.numpy as jnp
from jax import lax
from jax.experimental import pallas as pl
from jax.experimental.pallas import tpu as pltpu

f32 = jnp.float32
bf16 = jnp.bfloat16

D_MODEL = 1024
RWKV_HEAD = 64
RWKV_HEADS = 8
RWKV_W = 512
RWKV_LORA = 64
RWKV_PROJ = 3 * RWKV_W + 2 * RWKV_LORA
GLA_HEADS = 4
GLA_DK = 64
GLA_DV = 128
GLA_KW = GLA_HEADS * GLA_DK
GLA_VW = GLA_HEADS * GLA_DV
GLA_LORA = 16
GLA_LORA_PAD = 128
GLA_PROJ = 2 * GLA_KW + GLA_VW + GLA_LORA
EVEN_WIDTH = RWKV_W + GLA_VW
Q_OFF = RWKV_PROJ
K_OFF = Q_OFF + GLA_KW
V_OFF = K_OFF + GLA_KW
GD_OFF = V_OFF + GLA_VW
GATE_OFF = GD_OFF + GLA_LORA_PAD
EVEN_PROJ_PAD = GATE_OFF + EVEN_WIDTH
CONV_C = 1024
CONV_W = 31
CONV_HIST = CONV_W - 1
CONV_HIST_PAD = 32
ODD_PROJ = 3 * CONV_C

RMS_EPS = 1e-6
LN_EPS = 1e-5
RWKV_GN_EPS = 64e-5
GLA_GATE_NORM = 16.0
PROMPT_CHUNK = 64
SAMPLE_PAD_LEN = 8
PROMPT_SEQS_PER_STEP = 4
SAMPLE_SEQS_PER_STEP = 8
CONV_BLOCK = 128
CONV_ROW_TILE = 64
LANE = 128
SUBLANE = 8
VMEM_LIMIT_BYTES = 48 * 1024 * 1024

NN = ((1,), (0,))
NT = ((1,), (1,))
TN = ((0,), (0,))


def _dg(a, b, dims=NN):
    return lax.dot_general(a, b, (dims, ((), ())), preferred_element_type=f32)


def _dot1(a, b, dims=NN):
    return _dg(a.astype(bf16), b.astype(bf16), dims)


def _split2(x):
    hi = x.astype(bf16)
    lo = (x - hi.astype(f32)).astype(bf16)
    return hi, lo


def _split3(x):
    hi = x.astype(bf16)
    r1 = x - hi.astype(f32)
    mid = r1.astype(bf16)
    lo = (r1 - mid.astype(f32)).astype(bf16)
    return hi, mid, lo


def _dot3(a, b):
    ah, al = _split2(a)
    bh, bl = _split2(b)
    return _dg(ah, bh) + _dg(ah, bl) + _dg(al, bh)


def _dot_exact_lhs(m, x):
    hi, mid, lo = _split3(x)
    return _dg(m, hi) + _dg(m, mid) + _dg(m, lo)


def _dot_exact_rhs(x, m):
    hi, mid, lo = _split3(x)
    return _dg(hi, m) + _dg(mid, m) + _dg(lo, m)


def _sigmoid(x):
    return 1.0 / (1.0 + jnp.exp(-x))


def _log_sigmoid(x):
    return jnp.minimum(x, 0.0) - jnp.log(1.0 + jnp.exp(-jnp.abs(x)))


def _rmsnorm(x, g):
    return x * lax.rsqrt(jnp.mean(x * x, axis=-1, keepdims=True) + RMS_EPS) * g


def _inproj_kernel(x_ref, g_ref, w_ref, b_ref, o_ref):
    h = _rmsnorm(x_ref[...], g_ref[...])
    o_ref[...] = _dg(h.astype(bf16), w_ref[...]) + b_ref[...]


def _inproj(x2d, g, w, b):
    m, n = x2d.shape[0], w.shape[1]
    tm = 256
    return pl.pallas_call(
        _inproj_kernel,
        grid=(m // tm,),
        in_specs=[
            pl.BlockSpec((tm, D_MODEL), lambda i: (i, 0)),
            pl.BlockSpec((1, D_MODEL), lambda i: (0, 0)),
            pl.BlockSpec((D_MODEL, n), lambda i: (0, 0)),
            pl.BlockSpec((1, n), lambda i: (0, 0)),
        ],
        out_specs=pl.BlockSpec((tm, n), lambda i: (i, 0)),
        out_shape=jax.ShapeDtypeStruct((m, n), f32),
        compiler_params=pltpu.CompilerParams(
            dimension_semantics=("arbitrary",), vmem_limit_bytes=VMEM_LIMIT_BYTES),
        name="inproj",
    )(x2d, g, w, b)


def _outproj_kernel(o_ref, x_ref, w_ref, fn_ref, y_ref, *, final):
    y = x_ref[...] + _dg(o_ref[...].astype(bf16), w_ref[...])
    if final:
        y = _rmsnorm(y, fn_ref[...])
    y_ref[...] = y


def _outproj(o2d, x2d, w, fn, final):
    m = x2d.shape[0]
    tm = 512
    return pl.pallas_call(
        functools.partial(_outproj_kernel, final=final),
        grid=(m // tm,),
        in_specs=[
            pl.BlockSpec((tm, D_MODEL), lambda i: (i, 0)),
            pl.BlockSpec((tm, D_MODEL), lambda i: (i, 0)),
            pl.BlockSpec((D_MODEL, D_MODEL), lambda i: (0, 0)),
            pl.BlockSpec((1, D_MODEL), lambda i: (0, 0)),
        ],
        out_specs=pl.BlockSpec((tm, D_MODEL), lambda i: (i, 0)),
        out_shape=jax.ShapeDtypeStruct((m, D_MODEL), f32),
        compiler_params=pltpu.CompilerParams(
            dimension_semantics=("arbitrary",), vmem_limit_bytes=VMEM_LIMIT_BYTES),
        name="outproj",
    )(o2d, x2d, w, fn)


def _lower_left_masks(t, row, col):
    masks = []
    blk = 1
    while blk < t:
        masks.append(((row // (2 * blk)) == (col // (2 * blk)))
                     & ((row // blk) % 2 == 1) & ((col // blk) % 2 == 0))
        blk *= 2
    return masks


def _even_mix_kernel(p_ref, shift_ref, sr_in_ref, sg_in_ref, mu_ref, w0_ref, wup_ref, a0_ref, aup_ref,
                     kk_ref, ka_ref, rk_ref, lnw_ref, lnb_ref, gup_ref, gb_ref, gn_ref, bd_ref,
                     og_ref, sr_ref, sg_ref, prev_ref, *, t, n_valid, nb):
    c = pl.program_id(1)

    @pl.when(c == 0)
    def _():
        sr_ref[...] = sr_in_ref[...]
        sg_ref[...] = sg_in_ref[...]
        prev_ref[...] = shift_ref[...]

    row = lax.broadcasted_iota(jnp.int32, (t, t), 0)
    col = lax.broadcasted_iota(jnp.int32, (t, t), 1)
    strict = row > col
    incl = row >= col
    eye = jnp.where(row == col, 1.0, 0.0).astype(f32)
    tril = jnp.where(incl, 1.0, 0.0).astype(bf16)
    level_masks = _lower_left_masks(t, row, col)
    rowv = lax.broadcasted_iota(jnp.int32, (t, 1), 0)
    valid = None if n_valid == t else rowv < n_valid
    bd = bd_ref[...]
    rk64 = lax.broadcasted_iota(jnp.int32, (GLA_DK, GLA_DK), 0)
    ck64 = lax.broadcasted_iota(jnp.int32, (GLA_DK, GLA_DK), 1)

    def head_sum(x):
        return _dot_exact_rhs(x, bd)

    pairs = [(i, h) for i in range(nb) for h in range(RWKV_HEADS)]
    gpairs = [(i, h) for i in range(nb) for h in range(GLA_HEADS)]

    seq = []
    for i in range(nb):
        pr = p_ref[i, :, 0:RWKV_PROJ]
        p_prev = jnp.where(rowv == 0, prev_ref[i], pltpu.roll(pr, 1, 0))
        prev_ref[i] = pr[t - 1:t, :]
        xm = pr + (p_prev - pr) * mu_ref[...]
        r = xm[:, 0:RWKV_W]
        k = xm[:, RWKV_W:2 * RWKV_W]
        v = xm[:, 2 * RWKV_W:3 * RWKV_W]
        xwa = xm[:, 3 * RWKV_W:RWKV_PROJ]
        wl = w0_ref[...] + _dot3(jnp.tanh(xwa), wup_ref[...])
        al = a0_ref[...] + _dot3(xwa, aup_ref[...])
        ld = (-math.exp(-0.5)) * _sigmoid(wl)
        a = _sigmoid(al)
        kkr = k * kk_ref[...]
        kkn = kkr / jnp.maximum(jnp.sqrt(head_sum(kkr * kkr)), 1e-12)
        k2 = k * (1.0 + (a - 1.0) * ka_ref[...])
        if valid is not None:
            ld = jnp.where(valid, ld, 0.0)
            kkn = jnp.where(valid, kkn, 0.0)
            k2 = jnp.where(valid, k2, 0.0)
        c_in = _dot_exact_lhs(tril, ld)
        g_in = jnp.exp(c_in)
        ginv = jnp.exp(-c_in)
        seq.append(dict(r=r, k2=k2, v=v, g_last=g_in[t - 1:t, :],
                        abar=-kkn * jnp.exp(c_in - ld), bbar=kkn * a * ginv, kbar=k2 * ginv, rbar=r * g_in))

    hs = lambda h: slice(h * RWKV_HEAD, (h + 1) * RWKV_HEAD)
    lm = {(i, h): jnp.concatenate([seq[i]["abar"][:, hs(h)], seq[i]["rbar"][:, hs(h)]], axis=0).astype(bf16)
          for i, h in pairs}
    rm = {(i, h): jnp.concatenate([seq[i]["bbar"][:, hs(h)], seq[i]["kbar"][:, hs(h)]], axis=0).astype(bf16)
          for i, h in pairs}
    vh = {(i, h): seq[i]["v"][:, hs(h)] for i, h in pairs}
    gm = {ph: _dg(lm[ph], rm[ph], NT) for ph in pairs}
    a_ab = {ph: jnp.where(strict, gm[ph][:t, :t], 0.0) for ph in pairs}
    a_ak = {ph: jnp.where(strict, gm[ph][:t, t:], 0.0).astype(bf16) for ph in pairs}
    a_rbk = {ph: jnp.concatenate([jnp.where(incl, gm[ph][t:, :t], 0.0), jnp.where(incl, gm[ph][t:, t:], 0.0)],
                                 axis=1).astype(bf16) for ph in pairs}
    minv = {ph: eye + jnp.where(level_masks[0], a_ab[ph], 0.0) for ph in pairs}
    for mask in level_masks[1:]:
        mb = {ph: minv[ph].astype(bf16) for ph in pairs}
        tmp = {ph: _dg(mb[ph], jnp.where(mask, a_ab[ph], 0.0).astype(bf16)) for ph in pairs}
        minv = {ph: minv[ph] + _dg(tmp[ph].astype(bf16), mb[ph]) for ph in pairs}
    akv = {ph: _dg(a_ak[ph], vh[ph].astype(bf16)) for ph in pairs}
    s_old = {(i, h): sr_ref[i, h] for i, h in pairs}
    ls = {ph: _dg(lm[ph], s_old[ph].astype(bf16), NT) for ph in pairs}
    z = {ph: _dg(minv[ph].astype(bf16), (ls[ph][:t] + akv[ph]).astype(bf16)) for ph in pairs}
    zv = {ph: jnp.concatenate([z[ph], vh[ph]], axis=0).astype(bf16) for ph in pairs}
    yh = {ph: ls[ph][t:] + _dg(a_rbk[ph], zv[ph]) for ph in pairs}
    for i, h in pairs:
        sr_ref[i, h] = (s_old[i, h] + _dg(zv[i, h], rm[i, h], TN)) * seq[i]["g_last"][:, hs(h)]

    gseq = []
    for i in range(nb):
        q = p_ref[i, :, Q_OFF:K_OFF]
        kg = p_ref[i, :, K_OFF:V_OFF]
        gd = p_ref[i, :, GD_OFF:GATE_OFF]
        la = _log_sigmoid(_dot3(gd, gup_ref[...]) + gb_ref[...]) * (1.0 / GLA_GATE_NORM)
        if valid is not None:
            la = jnp.where(valid, la, 0.0)
            kg = jnp.where(valid, kg, 0.0)
        b = _dot_exact_lhs(tril, la)
        b_last = b[t - 1:t, :]
        gseq.append(dict(qg=(q * jnp.exp(b) * (GLA_DK ** -0.5)).astype(bf16), kgi=(kg * jnp.exp(-b)).astype(bf16),
                         kd=(kg * jnp.exp(b_last - b)).astype(bf16), eb=jnp.exp(b_last)))
    gs = lambda h: slice(h * GLA_DK, (h + 1) * GLA_DK)
    gv = {(i, h): p_ref[i, :, V_OFF + h * GLA_DV:V_OFF + (h + 1) * GLA_DV].astype(bf16) for i, h in gpairs}
    att = {(i, h): jnp.where(incl, _dg(gseq[i]["qg"][:, gs(h)], gseq[i]["kgi"][:, gs(h)], NT), 0.0).astype(bf16)
           for i, h in gpairs}
    gs_old = {(i, h): sg_ref[i, h] for i, h in gpairs}
    go = {(i, h): _dg(att[i, h], gv[i, h]) + _dg(gseq[i]["qg"][:, gs(h)], gs_old[i, h].astype(bf16))
          for i, h in gpairs}
    for i, h in gpairs:
        e_col = jnp.sum(jnp.where(rk64 == ck64, gseq[i]["eb"][:, gs(h)], 0.0), axis=1, keepdims=True)
        sg_ref[i, h] = e_col * gs_old[i, h] + _dg(gseq[i]["kd"][:, gs(h)], gv[i, h], TN)

    inv_k = 1.0 / RWKV_HEAD
    for i in range(nb):
        y = jnp.concatenate([yh[i, h] for h in range(RWKV_HEADS)], axis=1)
        mean = head_sum(y) * inv_k
        dlt = y - mean
        var = head_sum(dlt * dlt) * inv_k
        yn = dlt * lax.rsqrt(var + RWKV_GN_EPS) * lnw_ref[...] + lnb_ref[...]
        o_r = yn + head_sum(seq[i]["r"] * seq[i]["k2"] * rk_ref[...]) * seq[i]["v"]
        o_g = jnp.concatenate([_rmsnorm(go[i, h], gn_ref[...]) for h in range(GLA_HEADS)], axis=1)
        gate = p_ref[i, :, GATE_OFF:EVEN_PROJ_PAD]
        og_ref[i] = jnp.concatenate([o_r, o_g], axis=1) * (gate * _sigmoid(gate))


def _even_mix(p, shift0, sr0, sg0, wts, t, n_valid, nb):
    bsz, lp, _ = p.shape
    nc = lp // t
    const = lambda shape: pl.BlockSpec(shape, lambda b, c: (0,) * len(shape))
    return pl.pallas_call(
        functools.partial(_even_mix_kernel, t=t, n_valid=n_valid, nb=nb),
        grid=(bsz // nb, nc),
        in_specs=[
            pl.BlockSpec((nb, t, EVEN_PROJ_PAD), lambda b, c: (b, c, 0)),
            pl.BlockSpec((nb, 1, RWKV_PROJ), lambda b, c: (b, 0, 0)),
            pl.BlockSpec((nb, RWKV_HEADS, RWKV_HEAD, RWKV_HEAD), lambda b, c: (b, 0, 0, 0)),
            pl.BlockSpec((nb, GLA_HEADS, GLA_DK, GLA_DV), lambda b, c: (b, 0, 0, 0)),
            const((1, RWKV_PROJ)),
            const((1, RWKV_W)), const((2 * RWKV_LORA, RWKV_W)),
            const((1, RWKV_W)), const((2 * RWKV_LORA, RWKV_W)),
            const((1, RWKV_W)), const((1, RWKV_W)), const((1, RWKV_W)),
            const((1, RWKV_W)), const((1, RWKV_W)),
            const((GLA_LORA_PAD, GLA_KW)), const((1, GLA_KW)), const((1, GLA_DV)),
            const((RWKV_W, RWKV_W)),
        ],
        out_specs=[
            pl.BlockSpec((nb, t, EVEN_WIDTH), lambda b, c: (b, c, 0)),
            pl.BlockSpec((nb, RWKV_HEADS, RWKV_HEAD, RWKV_HEAD), lambda b, c: (b, 0, 0, 0)),
            pl.BlockSpec((nb, GLA_HEADS, GLA_DK, GLA_DV), lambda b, c: (b, 0, 0, 0)),
        ],
        out_shape=[
            jax.ShapeDtypeStruct((bsz, lp, EVEN_WIDTH), f32),
            jax.ShapeDtypeStruct((bsz, RWKV_HEADS, RWKV_HEAD, RWKV_HEAD), f32),
            jax.ShapeDtypeStruct((bsz, GLA_HEADS, GLA_DK, GLA_DV), f32),
        ],
        scratch_shapes=[pltpu.VMEM((nb, 1, RWKV_PROJ), f32)],
        compiler_params=pltpu.CompilerParams(
            dimension_semantics=("arbitrary", "arbitrary"), vmem_limit_bytes=VMEM_LIMIT_BYTES),
        name="even_mix",
    )(p, shift0, sr0, sg0, *wts)


def _odd_mix_kernel(p_ref, cache_ref, cw_ref, cb_ref, lnw_ref, lnb_ref,
                    o_ref, cache_out_ref, base_ref, ext_ref, y_ref, *, tb, n_valid):
    c = pl.program_id(1)
    pad = CONV_HIST_PAD - CONV_HIST
    rows = CONV_HIST_PAD + tb

    @pl.when(c == 0)
    def _():
        base_ref[0:SUBLANE, :] = jnp.zeros((SUBLANE, CONV_C), f32)
        base_ref[pad:CONV_HIST_PAD, :] = cache_ref[0]
        base_ref[rows:rows + SUBLANE, :] = jnp.zeros((SUBLANE, CONV_C), f32)

    base_ref[CONV_HIST_PAD:rows, :] = p_ref[0, :, 0:CONV_C] * _sigmoid(p_ref[0, :, CONV_C:2 * CONV_C])
    for sh in range(SUBLANE):
        for ct in range(CONV_C // LANE):
            ext_ref[sh, ct] = base_ref[sh:sh + rows, ct * LANE:(ct + 1) * LANE]

    row_tile = min(CONV_ROW_TILE, tb)
    for rt in range(tb // row_tile):
        r0 = rt * row_tile
        for ct in range(CONV_C // LANE):
            ls = slice(ct * LANE, (ct + 1) * LANE)
            acc = jnp.broadcast_to(cb_ref[:, ls], (row_tile, LANE))
            for sh in range(SUBLANE):
                taps = [j for j in range(CONV_W) if (pad + j) % SUBLANE == sh]
                lo = (pad + taps[0]) - sh
                hi = (pad + taps[-1]) - sh + row_tile
                window = ext_ref[sh, ct, r0 + lo:r0 + hi, :]
                for j in taps:
                    a0 = (pad + j) - sh - lo
                    acc = acc + cw_ref[j:j + 1, ls] * window[a0:a0 + row_tile]
            y_ref[r0:r0 + row_tile, ls] = acc

    y = y_ref[...]
    mu = jnp.mean(y, axis=-1, keepdims=True)
    d = y - mu
    var = jnp.mean(d * d, axis=-1, keepdims=True)
    yn = d * lax.rsqrt(var + LN_EPS) * lnw_ref[...] + lnb_ref[...]
    g = p_ref[0, :, 2 * CONV_C:3 * CONV_C]
    o_ref[0] = (yn * _sigmoid(yn)) * (g * _sigmoid(g))

    @pl.when(c == pl.num_programs(1) - 1)
    def _():
        cache_out_ref[0] = base_ref[pad + n_valid:pad + n_valid + CONV_HIST, :]

    base_ref[0:CONV_HIST_PAD, :] = base_ref[tb:rows, :]


def _odd_mix(p, cache, cw, cb, lnw, lnb, tb, n_valid):
    bsz, lp, _ = p.shape
    nc = lp // tb
    const = lambda shape: pl.BlockSpec(shape, lambda b, c: (0,) * len(shape))
    return pl.pallas_call(
        functools.partial(_odd_mix_kernel, tb=tb, n_valid=n_valid),
        grid=(bsz, nc),
        in_specs=[
            pl.BlockSpec((1, tb, ODD_PROJ), lambda b, c: (b, c, 0)),
            pl.BlockSpec((1, CONV_HIST, CONV_C), lambda b, c: (b, 0, 0)),
            const((CONV_W, CONV_C)), const((1, CONV_C)), const((1, CONV_C)), const((1, CONV_C)),
        ],
        out_specs=[
            pl.BlockSpec((1, tb, CONV_C), lambda b, c: (b, c, 0)),
            pl.BlockSpec((1, CONV_HIST, CONV_C), lambda b, c: (b, 0, 0)),
        ],
        out_shape=[
            jax.ShapeDtypeStruct((bsz, lp, CONV_C), f32),
            jax.ShapeDtypeStruct((bsz, CONV_HIST, CONV_C), f32),
        ],
        scratch_shapes=[pltpu.VMEM((CONV_HIST_PAD + tb + SUBLANE, CONV_C), f32),
                        pltpu.VMEM((SUBLANE, CONV_C // LANE, CONV_HIST_PAD + tb, LANE), f32),
                        pltpu.VMEM((tb, CONV_C), f32)],
        compiler_params=pltpu.CompilerParams(
            dimension_semantics=("arbitrary", "arbitrary"), vmem_limit_bytes=VMEM_LIMIT_BYTES),
        name="odd_mix",
    )(p, cache, cw, cb, lnw, lnb)


def _row(x):
    return x.reshape(1, -1)


def kernel(x_prompt, x_sample, state_rwkv_shift, state_rwkv, state_gla, cache_conv, norm_even, w_in_even, rwkv_mu, rwkv_w0, rwkv_w_up, rwkv_a0, rwkv_a_up, rwkv_k_k, rwkv_k_a, rwkv_r_k, rwkv_ln_w, rwkv_ln_b, gla_g_up, gla_g_b, gla_norm, w_out_even, norm_odd, w_in_odd, b_in_odd, conv_w, conv_b, conv_ln_w, conv_ln_b, w_out_odd, final_norm):
    bp, lp, _ = x_prompt.shape
    bs, ls, _ = x_sample.shape
    depth = norm_even.shape[0] + norm_odd.shape[0]
    xs_pad = jnp.pad(x_sample, ((0, 0), (0, SAMPLE_PAD_LEN - ls), (0, 0)))
    groups = [
        dict(x=x_prompt.reshape(bp * lp, D_MODEL), b=bp, l=lp, nv=lp, t=PROMPT_CHUNK, tb=CONV_BLOCK, prompt=True,
             nb=PROMPT_SEQS_PER_STEP),
        dict(x=xs_pad.reshape(bs * SAMPLE_PAD_LEN, D_MODEL), b=bs, l=SAMPLE_PAD_LEN, nv=ls, t=SAMPLE_PAD_LEN,
             tb=SAMPLE_PAD_LEN, prompt=False, nb=SAMPLE_SEQS_PER_STEP),
    ]
    bd = (jnp.arange(RWKV_W)[:, None] // RWKV_HEAD == jnp.arange(RWKV_W)[None, :] // RWKV_HEAD).astype(bf16)
    zeros_lora = jnp.zeros((RWKV_LORA, RWKV_W), f32)
    outs = [dict(shift=[], sr=[], sg=[], conv=[]) for _ in groups]

    for layer in range(depth):
        i = layer // 2
        final = layer == depth - 1
        if layer % 2 == 0:
            w = w_in_even[i]
            w_pad = jnp.concatenate(
                [w[:, :RWKV_PROJ + GLA_PROJ], jnp.zeros((D_MODEL, GLA_LORA_PAD - GLA_LORA), f32),
                 w[:, RWKV_PROJ + GLA_PROJ:]], axis=1).astype(bf16)
            b_pad = jnp.zeros((1, EVEN_PROJ_PAD), f32)
            wts = (
                _row(rwkv_mu[i]), _row(rwkv_w0[i]), jnp.concatenate([rwkv_w_up[i], zeros_lora], axis=0),
                _row(rwkv_a0[i]), jnp.concatenate([zeros_lora, rwkv_a_up[i]], axis=0),
                _row(rwkv_k_k[i]), _row(rwkv_k_a[i]), _row(rwkv_r_k[i]), _row(rwkv_ln_w[i]), _row(rwkv_ln_b[i]),
                jnp.concatenate([gla_g_up[i], jnp.zeros((GLA_LORA_PAD - GLA_LORA, GLA_KW), f32)], axis=0),
                _row(gla_g_b[i]), _row(gla_norm[i]), bd,
            )
            w_out = w_out_even[i].astype(bf16)
            for gi, g in enumerate(groups):
                p = _inproj(g["x"], _row(norm_even[i]), w_pad, b_pad).reshape(g["b"], g["l"], EVEN_PROJ_PAD)
                if g["prompt"]:
                    shift0 = jnp.zeros((g["b"], 1, RWKV_PROJ), f32)
                    sr0 = jnp.zeros((g["b"], RWKV_HEADS, RWKV_HEAD, RWKV_HEAD), f32)
                    sg0 = jnp.zeros((g["b"], GLA_HEADS, GLA_DK, GLA_DV), f32)
                else:
                    shift0 = state_rwkv_shift[i][:, None, :]
                    sr0 = state_rwkv[i]
                    sg0 = state_gla[i]
                og, sr, sg = _even_mix(p, shift0, sr0, sg0, wts, g["t"], min(g["nv"], g["t"]), g["nb"])
                outs[gi]["shift"].append(p[:, g["nv"] - 1, :RWKV_PROJ])
                outs[gi]["sr"].append(sr)
                outs[gi]["sg"].append(sg)
                g["x"] = _outproj(og.reshape(-1, EVEN_WIDTH), g["x"], w_out, _row(final_norm), final)
        else:
            w = w_in_odd[i].astype(bf16)
            w_out = w_out_odd[i].astype(bf16)
            for gi, g in enumerate(groups):
                p = _inproj(g["x"], _row(norm_odd[i]), w, _row(b_in_odd[i])).reshape(g["b"], g["l"], ODD_PROJ)
                cache = jnp.zeros((g["b"], CONV_HIST, CONV_C), f32) if g["prompt"] else cache_conv[i]
                y, cache_new = _odd_mix(p, cache, conv_w[i], _row(conv_b[i]), _row(conv_ln_w[i]),
                                        _row(conv_ln_b[i]), g["tb"], min(g["nv"], g["tb"]))
                outs[gi]["conv"].append(cache_new)
                g["x"] = _outproj(y.reshape(-1, CONV_C), g["x"], w_out, _row(final_norm), final)

    y_prompt = groups[0]["x"].reshape(bp, lp, D_MODEL)
    y_sample = groups[1]["x"].reshape(bs, SAMPLE_PAD_LEN, D_MODEL)[:, :ls]
    op, os_ = outs
    return (y_prompt, y_sample,
            jnp.stack(op["shift"]), jnp.stack(os_["shift"]),
            jnp.stack(op["sr"]), jnp.stack(os_["sr"]),
            jnp.stack(op["sg"]), jnp.stack(os_["sg"]),
            jnp.stack(op["conv"]), jnp.stack(os_["conv"]))
```

```python
import functools
import math

import jax
import jax.numpy as jnp
import numpy as np
from jax import lax
from jax.experimental import pallas as pl
from jax.experimental.pallas import tpu as pltpu

f32 = jnp.float32
bf16 = jnp.bfloat16

D_MODEL = 1024
RWKV_HEAD = 64
RWKV_HEADS = 8
RWKV_W = 512
RWKV_LORA = 64
RWKV_PROJ = 3 * RWKV_W + 2 * RWKV_LORA
GLA_HEADS = 4
GLA_DK = 64
GLA_DV = 128
GLA_KW = GLA_HEADS * GLA_DK
GLA_VW = GLA_HEADS * GLA_DV
GLA_LORA = 16
GLA_LORA_PAD = 128
GLA_PROJ = 2 * GLA_KW + GLA_VW + GLA_LORA
EVEN_WIDTH = RWKV_W + GLA_VW
Q_OFF = RWKV_PROJ
K_OFF = Q_OFF + GLA_KW
V_OFF = K_OFF + GLA_KW
GD_OFF = V_OFF + GLA_VW
GATE_OFF = GD_OFF + GLA_LORA_PAD
EVEN_PROJ_PAD = GATE_OFF + EVEN_WIDTH
CONV_C = 1024
CONV_W = 31
CONV_HIST = CONV_W - 1
CONV_HIST_PAD = 32
ODD_PROJ = 3 * CONV_C

RMS_EPS = 1e-6
LN_EPS = 1e-5
RWKV_GN_EPS = 64e-5
GLA_GATE_NORM = 16.0
PROMPT_CHUNK = 64
SAMPLE_PAD_LEN = 8
PROMPT_SEQS_PER_STEP = 4
SAMPLE_SEQS_PER_STEP = 8
CONV_BLOCK = 128
CONV_ROW_TILE = 64
LANE = 128
SUBLANE = 8
BF16_SUBLANES = 16
HEAD_GROUP = 4
VMEM_LIMIT_BYTES = 48 * 1024 * 1024

NN = ((1,), (0,))
NT = ((1,), (1,))
TN = ((0,), (0,))


def _dg(a, b, dims=NN):
    return lax.dot_general(a, b, (dims, ((), ())), preferred_element_type=f32)


def _dot1(a, b, dims=NN):
    return _dg(a.astype(bf16), b.astype(bf16), dims)


def _split2(x):
    hi = x.astype(bf16)
    lo = (x - hi.astype(f32)).astype(bf16)
    return hi, lo


def _split3(x):
    hi = x.astype(bf16)
    r1 = x - hi.astype(f32)
    mid = r1.astype(bf16)
    lo = (r1 - mid.astype(f32)).astype(bf16)
    return hi, mid, lo


def _dot3(a, b):
    ah, al = _split2(a)
    bh, bl = _split2(b)
    return _dg(ah, bh) + _dg(ah, bl) + _dg(al, bh)


def _dot_exact_lhs(m, x):
    hi, mid, lo = _split3(x)
    return _dg(m, hi) + _dg(m, mid) + _dg(m, lo)


def _dot_exact_rhs(x, m):
    hi, mid, lo = _split3(x)
    return _dg(hi, m) + _dg(mid, m) + _dg(lo, m)


def _sigmoid(x):
    return 1.0 / (1.0 + jnp.exp(-x))


def _log_sigmoid(x):
    return jnp.minimum(x, 0.0) - jnp.log(1.0 + jnp.exp(-jnp.abs(x)))


def _rmsnorm(x, g):
    return x * lax.rsqrt(jnp.mean(x * x, axis=-1, keepdims=True) + RMS_EPS) * g


def _inproj_kernel(x_ref, g_ref, w_ref, b_ref, o_ref):
    h = _rmsnorm(x_ref[...], g_ref[...])
    o_ref[...] = _dg(h.astype(bf16), w_ref[...]) + b_ref[...]


def _inproj(x2d, g, w, b):
    m, n = x2d.shape[0], w.shape[1]
    tm = 256
    return pl.pallas_call(
        _inproj_kernel,
        grid=(m // tm,),
        in_specs=[
            pl.BlockSpec((tm, D_MODEL), lambda i: (i, 0)),
            pl.BlockSpec((1, D_MODEL), lambda i: (0, 0)),
            pl.BlockSpec((D_MODEL, n), lambda i: (0, 0)),
            pl.BlockSpec((1, n), lambda i: (0, 0)),
        ],
        out_specs=pl.BlockSpec((tm, n), lambda i: (i, 0)),
        out_shape=jax.ShapeDtypeStruct((m, n), f32),
        compiler_params=pltpu.CompilerParams(
            dimension_semantics=("arbitrary",), vmem_limit_bytes=VMEM_LIMIT_BYTES),
        name="inproj",
    )(x2d, g, w, b)


def _outproj_kernel(o_ref, x_ref, w_ref, fn_ref, y_ref, *, final):
    y = x_ref[...] + _dg(o_ref[...].astype(bf16), w_ref[...])
    if final:
        y = _rmsnorm(y, fn_ref[...])
    y_ref[...] = y


def _outproj(o2d, x2d, w, fn, final):
    m = x2d.shape[0]
    tm = 512
    return pl.pallas_call(
        functools.partial(_outproj_kernel, final=final),
        grid=(m // tm,),
        in_specs=[
            pl.BlockSpec((tm, D_MODEL), lambda i: (i, 0)),
            pl.BlockSpec((tm, D_MODEL), lambda i: (i, 0)),
            pl.BlockSpec((D_MODEL, D_MODEL), lambda i: (0, 0)),
            pl.BlockSpec((1, D_MODEL), lambda i: (0, 0)),
        ],
        out_specs=pl.BlockSpec((tm, D_MODEL), lambda i: (i, 0)),
        out_shape=jax.ShapeDtypeStruct((m, D_MODEL), f32),
        compiler_params=pltpu.CompilerParams(
            dimension_semantics=("arbitrary",), vmem_limit_bytes=VMEM_LIMIT_BYTES),
        name="outproj",
    )(o2d, x2d, w, fn)


N_EVEN_WEIGHTS = 16


def _even_mix_kernel(*refs, t, n_valid, nb, has_state, n_alias):
    p_ref = refs[0]
    n_in = 1 + (3 if has_state else 0)
    (mu_ref, w0_ref, wup_ref, a0_ref, aup_ref, kk_ref, ka_ref, rk_ref, lnw_ref, lnb_ref, gup_ref, gb_ref, gn_ref,
     bd_ref, fmask_ref, tmask_ref) = refs[n_in:n_in + N_EVEN_WEIGHTS]
    og_ref, sr_ref, sg_ref, prev_ref, srs_ref = refs[n_in + N_EVEN_WEIGHTS + n_alias:]
    c = pl.program_id(1)
    n_groups = RWKV_HEADS // HEAD_GROUP

    @pl.when(c == 0)
    def _():
        if not has_state:
            sg_ref[...] = jnp.zeros(sg_ref.shape, f32)
            prev_ref[...] = jnp.zeros(prev_ref.shape, f32)
            srs_ref[...] = jnp.zeros(srs_ref.shape, f32)
            return
        shift_ref, sr_in_ref, sg_in_ref = refs[1:4]
        sg_ref[...] = sg_in_ref[...]
        prev_ref[...] = shift_ref[...]
        for i in range(nb):
            for q in range(n_groups):
                rows_ = []
                for j in range(HEAD_GROUP):
                    parts = [jnp.zeros((RWKV_HEAD, RWKV_HEAD), f32)] * HEAD_GROUP
                    parts[j] = sr_in_ref[i, q * HEAD_GROUP + j]
                    rows_.append(jnp.concatenate(parts, axis=1))
                srs_ref[i, q] = jnp.concatenate(rows_, axis=0)

    row = lax.broadcasted_iota(jnp.int32, (t, t), 0)
    col = lax.broadcasted_iota(jnp.int32, (t, t), 1)
    incl = row >= col
    tril = jnp.where(incl, 1.0, 0.0).astype(bf16)
    rowv = lax.broadcasted_iota(jnp.int32, (t, 1), 0)
    valid = None if n_valid == t else rowv < n_valid
    bd = bd_ref[...]
    rk64 = lax.broadcasted_iota(jnp.int32, (GLA_DK, GLA_DK), 0)
    ck64 = lax.broadcasted_iota(jnp.int32, (GLA_DK, GLA_DK), 1)

    def head_sum(x):
        tiles = x.shape[1] // LANE
        stacked = jnp.concatenate([x[:, m * LANE:(m + 1) * LANE] for m in range(tiles)], axis=0)
        s = _dot_exact_rhs(stacked, bd)
        return jnp.concatenate([s[m * t:(m + 1) * t] for m in range(tiles)], axis=1)

    gpairs = [(i, h) for i in range(nb) for h in range(GLA_HEADS)]

    seq = []
    for i in range(nb):
        pr = p_ref[i, :, 0:RWKV_PROJ]
        p_prev = jnp.where(rowv == 0, prev_ref[i], pltpu.roll(pr, 1, 0))
        prev_ref[i] = pr[t - 1:t, :]
        xm = pr + (p_prev - pr) * mu_ref[...]
        r = xm[:, 0:RWKV_W]
        k = xm[:, RWKV_W:2 * RWKV_W]
        v = xm[:, 2 * RWKV_W:3 * RWKV_W]
        xwa = xm[:, 3 * RWKV_W:RWKV_PROJ]
        wl = w0_ref[...] + _dot3(jnp.tanh(xwa), wup_ref[...])
        al = a0_ref[...] + _dot3(xwa, aup_ref[...])
        ld = (-math.exp(-0.5)) * _sigmoid(wl)
        a = _sigmoid(al)
        kkr = k * kk_ref[...]
        kkn = kkr / jnp.maximum(jnp.sqrt(head_sum(kkr * kkr)), 1e-12)
        k2 = k * (1.0 + (a - 1.0) * ka_ref[...])
        if valid is not None:
            ld = jnp.where(valid, ld, 0.0)
            kkn = jnp.where(valid, kkn, 0.0)
            k2 = jnp.where(valid, k2, 0.0)
        c_in = _dot_exact_lhs(tril, ld)
        g_in = jnp.exp(c_in)
        ginv = jnp.exp(-c_in)
        seq.append(dict(r=r, k2=k2, v=v, g_last=g_in[t - 1:t, :],
                        abar=-kkn * jnp.exp(c_in - ld), bbar=kkn * a * ginv, kbar=k2 * ginv, rbar=r * g_in))

    fw = HEAD_GROUP * RWKV_HEAD
    tw = HEAD_GROUP * t
    rt = lax.broadcasted_iota(jnp.int32, (t, tw), 0)
    st = lax.broadcasted_iota(jnp.int32, (t, tw), 1) % t
    strict_g = rt > st
    incl_g = rt >= st
    eye_g = jnp.where(rt == st, 1.0, 0.0).astype(f32)
    first_level_g = (rt // 2 == st // 2) & (rt % 2 == 1) & (st % 2 == 0)
    rs = lax.broadcasted_iota(jnp.int32, (fw, fw), 0) // RWKV_HEAD
    cs = lax.broadcasted_iota(jnp.int32, (fw, fw), 1) // RWKV_HEAD
    state_diag = rs == cs

    def block_diag(x, mask):
        if x.shape[0] % BF16_SUBLANES == 0:
            xb = x.astype(bf16)
            return jnp.concatenate([xb] * HEAD_GROUP, axis=0) * mask
        return (jnp.concatenate([x] * HEAD_GROUP, axis=0) * mask.astype(f32)).astype(bf16)

    fmask = fmask_ref[...]
    groups = [(i, q) for i in range(nb) for q in range(RWKV_HEADS // HEAD_GROUP)]
    fs = lambda q: slice(q * fw, (q + 1) * fw)
    lm = {(i, q): jnp.concatenate([seq[i]["abar"][:, fs(q)], seq[i]["rbar"][:, fs(q)]], axis=0).astype(bf16)
          for i, q in groups}
    vq = {(i, q): seq[i]["v"][:, fs(q)] for i, q in groups}
    vbd = {iq: block_diag(vq[iq], fmask) for iq in groups}
    gb = {(i, q): _dg(lm[i, q], block_diag(seq[i]["bbar"][:, fs(q)], fmask), NT) for i, q in groups}
    gk = {(i, q): _dg(lm[i, q], block_diag(seq[i]["kbar"][:, fs(q)], fmask), NT) for i, q in groups}
    a_ab = {iq: jnp.where(strict_g, gb[iq][:t], 0.0) for iq in groups}
    a_ak = {iq: jnp.where(strict_g, gk[iq][:t], 0.0).astype(bf16) for iq in groups}
    a_rb = {iq: jnp.where(incl_g, gb[iq][t:], 0.0).astype(bf16) for iq in groups}
    a_rk = {iq: jnp.where(incl_g, gk[iq][t:], 0.0).astype(bf16) for iq in groups}
    minv = {iq: eye_g + jnp.where(first_level_g, a_ab[iq], 0.0) for iq in groups}
    for lvl in range(1, tmask_ref.shape[0]):
        tmp = {iq: _dg(minv[iq].astype(bf16), block_diag(a_ab[iq], tmask_ref[lvl])) for iq in groups}
        minv = {iq: minv[iq] + _dg(tmp[iq].astype(bf16), block_diag(minv[iq], tmask_ref[0])) for iq in groups}
    akv = {iq: _dg(a_ak[iq], vbd[iq]) for iq in groups}
    s_old = {(i, q): srs_ref[i, q] for i, q in groups}
    ls = {iq: _dg(lm[iq], s_old[iq].astype(bf16), NT) for iq in groups}
    z = {iq: _dg(minv[iq].astype(bf16), block_diag(ls[iq][:t] + akv[iq], fmask)) for iq in groups}
    yq = {iq: ls[iq][t:] + _dg(a_rb[iq], block_diag(z[iq], fmask)) + _dg(a_rk[iq], vbd[iq]) for iq in groups}
    for i, q in groups:
        zv = jnp.concatenate([z[i, q], vq[i, q]], axis=0).astype(bf16)
        rm = jnp.concatenate([seq[i]["bbar"][:, fs(q)], seq[i]["kbar"][:, fs(q)]], axis=0).astype(bf16)
        upd = jnp.where(state_diag, _dg(zv, rm, TN), 0.0)
        srs_ref[i, q] = (s_old[i, q] + upd) * seq[i]["g_last"][:, fs(q)]

    gseq = []
    for i in range(nb):
        q = p_ref[i, :, Q_OFF:K_OFF]
        kg = p_ref[i, :, K_OFF:V_OFF]
        gd = p_ref[i, :, GD_OFF:GATE_OFF]
        la = _log_sigmoid(_dot3(gd, gup_ref[...]) + gb_ref[...]) * (1.0 / GLA_GATE_NORM)
        if valid is not None:
            la = jnp.where(valid, la, 0.0)
            kg = jnp.where(valid, kg, 0.0)
        b = _dot_exact_lhs(tril, la)
        b_last = b[t - 1:t, :]
        gseq.append(dict(qg=(q * jnp.exp(b) * (GLA_DK ** -0.5)).astype(bf16), kgi=(kg * jnp.exp(-b)).astype(bf16),
                         kd=(kg * jnp.exp(b_last - b)).astype(bf16), eb=jnp.exp(b_last)))
    gs = lambda h: slice(h * GLA_DK, (h + 1) * GLA_DK)
    gv = {(i, h): p_ref[i, :, V_OFF + h * GLA_DV:V_OFF + (h + 1) * GLA_DV].astype(bf16) for i, h in gpairs}
    att = {(i, h): jnp.where(incl, _dg(gseq[i]["qg"][:, gs(h)], gseq[i]["kgi"][:, gs(h)], NT), 0.0).astype(bf16)
           for i, h in gpairs}
    gs_old = {(i, h): sg_ref[i, h] for i, h in gpairs}
    go = {(i, h): _dg(att[i, h], gv[i, h]) + _dg(gseq[i]["qg"][:, gs(h)], gs_old[i, h].astype(bf16))
          for i, h in gpairs}
    for i, h in gpairs:
        e_col = jnp.sum(jnp.where(rk64 == ck64, gseq[i]["eb"][:, gs(h)], 0.0), axis=1, keepdims=True)
        sg_ref[i, h] = e_col * gs_old[i, h] + _dg(gseq[i]["kd"][:, gs(h)], gv[i, h], TN)

    inv_k = 1.0 / RWKV_HEAD
    for i in range(nb):
        y = jnp.concatenate([yq[i, q] for q in range(n_groups)], axis=1)
        mean = head_sum(y) * inv_k
        dlt = y - mean
        var = head_sum(dlt * dlt) * inv_k
        yn = dlt * lax.rsqrt(var + RWKV_GN_EPS) * lnw_ref[...] + lnb_ref[...]
        o_r = yn + head_sum(seq[i]["r"] * seq[i]["k2"] * rk_ref[...]) * seq[i]["v"]
        o_g = jnp.concatenate([_rmsnorm(go[i, h], gn_ref[...]) for h in range(GLA_HEADS)], axis=1)
        gate = p_ref[i, :, GATE_OFF:EVEN_PROJ_PAD]
        og_ref[i] = jnp.concatenate([o_r, o_g], axis=1) * (gate * _sigmoid(gate))

    @pl.when(c == pl.num_programs(1) - 1)
    def _():
        for i in range(nb):
            for q in range(n_groups):
                for j in range(HEAD_GROUP):
                    lo = j * RWKV_HEAD
                    sr_ref[i, q * HEAD_GROUP + j] = srs_ref[i, q, lo:lo + RWKV_HEAD, lo:lo + RWKV_HEAD]


def _group_masks(t):
    rows = np.arange(HEAD_GROUP * t)[:, None]
    fmask = rows // t == np.arange(HEAD_GROUP * RWKV_HEAD)[None, :] // RWKV_HEAD
    lanes = np.arange(HEAD_GROUP * t)[None, :]
    same_head = rows // t == lanes // t
    r, s = rows % t, lanes % t
    levels = [same_head]
    blk = 2
    while blk < t:
        levels.append(same_head & (r // (2 * blk) == s // (2 * blk)) & ((r // blk) % 2 == 1) & ((s // blk) % 2 == 0))
        blk *= 2
    return jnp.asarray(fmask, bf16), jnp.asarray(np.stack(levels), bf16)


def _even_mix(p, layer, n_layers, states, prev_outs, wts, t, n_valid, nb):
    bsz, lp, _ = p.shape
    nc = lp // t
    const = lambda shape: pl.BlockSpec(shape, lambda b, c: (0,) * len(shape))
    fmask, tmask = _group_masks(t)
    n_groups = RWKV_HEADS // HEAD_GROUP
    slab = HEAD_GROUP * RWKV_HEAD
    sr_block = (None, nb, RWKV_HEADS, RWKV_HEAD, RWKV_HEAD)
    sg_block = (None, nb, GLA_HEADS, GLA_DK, GLA_DV)
    layer_map = lambda b, c: (layer, b, 0, 0, 0)
    state_specs, state_args = [], []
    if states is not None:
        shift, sr0, sg0 = states
        state_specs = [pl.BlockSpec((None, nb, 1, RWKV_PROJ), lambda b, c: (layer, b, 0, 0)),
                       pl.BlockSpec(sr_block, layer_map), pl.BlockSpec(sg_block, layer_map)]
        state_args = [shift.reshape(shift.shape[0], bsz, 1, RWKV_PROJ), sr0, sg0]
    alias_specs, alias_args, aliases = [], [], {}
    if prev_outs is not None:
        alias_specs = [pl.BlockSpec(memory_space=pl.ANY)] * 2
        alias_args = list(prev_outs)
        first = 1 + len(state_args) + N_EVEN_WEIGHTS
        aliases = {first: 1, first + 1: 2}
    return pl.pallas_call(
        functools.partial(_even_mix_kernel, t=t, n_valid=n_valid, nb=nb, has_state=states is not None,
                          n_alias=len(alias_args)),
        grid=(bsz // nb, nc),
        in_specs=[
            pl.BlockSpec((nb, t, EVEN_PROJ_PAD), lambda b, c: (b, c, 0)),
            *state_specs,
            const((1, RWKV_PROJ)),
            const((1, RWKV_W)), const((2 * RWKV_LORA, RWKV_W)),
            const((1, RWKV_W)), const((2 * RWKV_LORA, RWKV_W)),
            const((1, RWKV_W)), const((1, RWKV_W)), const((1, RWKV_W)),
            const((1, RWKV_W)), const((1, RWKV_W)),
            const((GLA_LORA_PAD, GLA_KW)), const((1, GLA_KW)), const((1, GLA_DV)),
            const((LANE, LANE)), const(fmask.shape), const(tmask.shape),
            *alias_specs,
        ],
        out_specs=[
            pl.BlockSpec((nb, t, EVEN_WIDTH), lambda b, c: (b, c, 0)),
            pl.BlockSpec(sr_block, layer_map),
            pl.BlockSpec(sg_block, layer_map),
        ],
        out_shape=[
            jax.ShapeDtypeStruct((bsz, lp, EVEN_WIDTH), f32),
            jax.ShapeDtypeStruct((n_layers, bsz, RWKV_HEADS, RWKV_HEAD, RWKV_HEAD), f32),
            jax.ShapeDtypeStruct((n_layers, bsz, GLA_HEADS, GLA_DK, GLA_DV), f32),
        ],
        input_output_aliases=aliases,
        scratch_shapes=[pltpu.VMEM((nb, 1, RWKV_PROJ), f32), pltpu.VMEM((nb, n_groups, slab, slab), f32)],
        compiler_params=pltpu.CompilerParams(
            dimension_semantics=("arbitrary", "arbitrary"), vmem_limit_bytes=VMEM_LIMIT_BYTES),
        name="even_mix",
    )(p, *state_args, *wts, fmask, tmask, *alias_args)


def _odd_mix_kernel(*refs, tb, n_valid, has_cache, n_alias):
    p_ref = refs[0]
    n_in = 2 if has_cache else 1
    cw_ref, cb_ref, lnw_ref, lnb_ref = refs[n_in:n_in + 4]
    o_ref, cache_out_ref, base_ref, ext_ref, y_ref = refs[n_in + 4 + n_alias:]
    c = pl.program_id(1)
    pad = CONV_HIST_PAD - CONV_HIST
    rows = CONV_HIST_PAD + tb

    @pl.when(c == 0)
    def _():
        if has_cache:
            base_ref[0:SUBLANE, :] = jnp.zeros((SUBLANE, CONV_C), f32)
            base_ref[pad:CONV_HIST_PAD, :] = refs[1][0]
        else:
            base_ref[0:CONV_HIST_PAD, :] = jnp.zeros((CONV_HIST_PAD, CONV_C), f32)
        base_ref[rows:rows + SUBLANE, :] = jnp.zeros((SUBLANE, CONV_C), f32)

    base_ref[CONV_HIST_PAD:rows, :] = p_ref[0, :, 0:CONV_C] * _sigmoid(p_ref[0, :, CONV_C:2 * CONV_C])
    for sh in range(SUBLANE):
        for ct in range(CONV_C // LANE):
            ext_ref[sh, ct] = base_ref[sh:sh + rows, ct * LANE:(ct + 1) * LANE]

    row_tile = min(CONV_ROW_TILE, tb)
    for rt in range(tb // row_tile):
        r0 = rt * row_tile
        for ct in range(CONV_C // LANE):
            ls = slice(ct * LANE, (ct + 1) * LANE)
            acc = jnp.broadcast_to(cb_ref[:, ls], (row_tile, LANE))
            for sh in range(SUBLANE):
                taps = [j for j in range(CONV_W) if (pad + j) % SUBLANE == sh]
                lo = (pad + taps[0]) - sh
                hi = (pad + taps[-1]) - sh + row_tile
                window = ext_ref[sh, ct, r0 + lo:r0 + hi, :]
                for j in taps:
                    a0 = (pad + j) - sh - lo
                    acc = acc + cw_ref[j:j + 1, ls] * window[a0:a0 + row_tile]
            y_ref[r0:r0 + row_tile, ls] = acc

    y = y_ref[...]
    mu = jnp.mean(y, axis=-1, keepdims=True)
    d = y - mu
    var = jnp.mean(d * d, axis=-1, keepdims=True)
    yn = d * lax.rsqrt(var + LN_EPS) * lnw_ref[...] + lnb_ref[...]
    g = p_ref[0, :, 2 * CONV_C:3 * CONV_C]
    o_ref[0] = (yn * _sigmoid(yn)) * (g * _sigmoid(g))

    @pl.when(c == pl.num_programs(1) - 1)
    def _():
        cache_out_ref[0] = base_ref[pad + n_valid:pad + n_valid + CONV_HIST, :]

    base_ref[0:CONV_HIST_PAD, :] = base_ref[tb:rows, :]


def _odd_mix(p, layer, n_layers, cache, prev_out, cw, cb, lnw, lnb, tb, n_valid):
    bsz, lp, _ = p.shape
    nc = lp // tb
    const = lambda shape: pl.BlockSpec(shape, lambda b, c: (0,) * len(shape))
    cache_block = (None, 1, CONV_HIST, CONV_C)
    layer_map = lambda b, c: (layer, b, 0, 0)
    cache_specs = [] if cache is None else [pl.BlockSpec(cache_block, layer_map)]
    cache_args = [] if cache is None else [cache]
    alias_specs = [] if prev_out is None else [pl.BlockSpec(memory_space=pl.ANY)]
    alias_args = [] if prev_out is None else [prev_out]
    aliases = {} if prev_out is None else {1 + len(cache_args) + 4: 1}
    return pl.pallas_call(
        functools.partial(_odd_mix_kernel, tb=tb, n_valid=n_valid, has_cache=cache is not None,
                          n_alias=len(alias_args)),
        grid=(bsz, nc),
        in_specs=[
            pl.BlockSpec((1, tb, ODD_PROJ), lambda b, c: (b, c, 0)),
            *cache_specs,
            const((CONV_W, CONV_C)), const((1, CONV_C)), const((1, CONV_C)), const((1, CONV_C)),
            *alias_specs,
        ],
        out_specs=[
            pl.BlockSpec((1, tb, CONV_C), lambda b, c: (b, c, 0)),
            pl.BlockSpec(cache_block, layer_map),
        ],
        out_shape=[
            jax.ShapeDtypeStruct((bsz, lp, CONV_C), f32),
            jax.ShapeDtypeStruct((n_layers, bsz, CONV_HIST, CONV_C), f32),
        ],
        input_output_aliases=aliases,
        scratch_shapes=[pltpu.VMEM((CONV_HIST_PAD + tb + SUBLANE, CONV_C), f32),
                        pltpu.VMEM((SUBLANE, CONV_C // LANE, CONV_HIST_PAD + tb, LANE), f32),
                        pltpu.VMEM((tb, CONV_C), f32)],
        compiler_params=pltpu.CompilerParams(
            dimension_semantics=("arbitrary", "arbitrary"), vmem_limit_bytes=VMEM_LIMIT_BYTES),
        name="odd_mix",
    )(p, *cache_args, cw, cb, lnw, lnb, *alias_args)


def _row(x):
    return x.reshape(1, -1)


def kernel(x_prompt, x_sample, state_rwkv_shift, state_rwkv, state_gla, cache_conv, norm_even, w_in_even, rwkv_mu, rwkv_w0, rwkv_w_up, rwkv_a0, rwkv_a_up, rwkv_k_k, rwkv_k_a, rwkv_r_k, rwkv_ln_w, rwkv_ln_b, gla_g_up, gla_g_b, gla_norm, w_out_even, norm_odd, w_in_odd, b_in_odd, conv_w, conv_b, conv_ln_w, conv_ln_b, w_out_odd, final_norm):
    bp, lp, _ = x_prompt.shape
    bs, ls, _ = x_sample.shape
    depth = norm_even.shape[0] + norm_odd.shape[0]
    xs_pad = jnp.pad(x_sample, ((0, 0), (0, SAMPLE_PAD_LEN - ls), (0, 0)))
    groups = [
        dict(x=x_prompt.reshape(bp * lp, D_MODEL), b=bp, l=lp, nv=lp, t=PROMPT_CHUNK, tb=CONV_BLOCK, prompt=True,
             nb=PROMPT_SEQS_PER_STEP),
        dict(x=xs_pad.reshape(bs * SAMPLE_PAD_LEN, D_MODEL), b=bs, l=SAMPLE_PAD_LEN, nv=ls, t=SAMPLE_PAD_LEN,
             tb=SAMPLE_PAD_LEN, prompt=False, nb=SAMPLE_SEQS_PER_STEP),
    ]
    bd = (jnp.arange(LANE)[:, None] // RWKV_HEAD == jnp.arange(LANE)[None, :] // RWKV_HEAD).astype(bf16)
    zeros_lora = jnp.zeros((RWKV_LORA, RWKV_W), f32)
    outs = [dict(shift=[], states=None, conv=None) for _ in groups]
    n_even, n_odd = norm_even.shape[0], norm_odd.shape[0]

    for layer in range(depth):
        i = layer // 2
        final = layer == depth - 1
        if layer % 2 == 0:
            w = w_in_even[i]
            w_pad = jnp.concatenate(
                [w[:, :RWKV_PROJ + GLA_PROJ], jnp.zeros((D_MODEL, GLA_LORA_PAD - GLA_LORA), f32),
                 w[:, RWKV_PROJ + GLA_PROJ:]], axis=1).astype(bf16)
            b_pad = jnp.zeros((1, EVEN_PROJ_PAD), f32)
            wts = (
                _row(rwkv_mu[i]), _row(rwkv_w0[i]), jnp.concatenate([rwkv_w_up[i], zeros_lora], axis=0),
                _row(rwkv_a0[i]), jnp.concatenate([zeros_lora, rwkv_a_up[i]], axis=0),
                _row(rwkv_k_k[i]), _row(rwkv_k_a[i]), _row(rwkv_r_k[i]), _row(rwkv_ln_w[i]), _row(rwkv_ln_b[i]),
                jnp.concatenate([gla_g_up[i], jnp.zeros((GLA_LORA_PAD - GLA_LORA, GLA_KW), f32)], axis=0),
                _row(gla_g_b[i]), _row(gla_norm[i]), bd,
            )
            w_out = w_out_even[i].astype(bf16)
            for gi, g in enumerate(groups):
                p = _inproj(g["x"], _row(norm_even[i]), w_pad, b_pad).reshape(g["b"], g["l"], EVEN_PROJ_PAD)
                states = None if g["prompt"] else (state_rwkv_shift, state_rwkv, state_gla)
                og, sr, sg = _even_mix(p, i, n_even, states, outs[gi]["states"], wts, g["t"],
                                       min(g["nv"], g["t"]), g["nb"])
                outs[gi]["shift"].append(p[:, g["nv"] - 1, :RWKV_PROJ])
                outs[gi]["states"] = (sr, sg)
                g["x"] = _outproj(og.reshape(-1, EVEN_WIDTH), g["x"], w_out, _row(final_norm), final)
        else:
            w = w_in_odd[i].astype(bf16)
            w_out = w_out_odd[i].astype(bf16)
            for gi, g in enumerate(groups):
                p = _inproj(g["x"], _row(norm_odd[i]), w, _row(b_in_odd[i])).reshape(g["b"], g["l"], ODD_PROJ)
                cache = None if g["prompt"] else cache_conv
                y, outs[gi]["conv"] = _odd_mix(p, i, n_odd, cache, outs[gi]["conv"], conv_w[i], _row(conv_b[i]),
                                               _row(conv_ln_w[i]), _row(conv_ln_b[i]), g["tb"],
                                               min(g["nv"], g["tb"]))
                g["x"] = _outproj(y.reshape(-1, CONV_C), g["x"], w_out, _row(final_norm), final)

    y_prompt = groups[0]["x"].reshape(bp, lp, D_MODEL)
    y_sample = groups[1]["x"].reshape(bs, SAMPLE_PAD_LEN, D_MODEL)[:, :ls]
    op, os_ = outs
    return (y_prompt, y_sample,
            jnp.stack(op["shift"]), jnp.stack(os_["shift"]),
            op["states"][0], os_["states"][0],
            op["states"][1], os_["states"][1],
            op["conv"], os_["conv"])
```

```python
import functools
import math

import jax
import jax.numpy as jnp
import numpy as np
from jax import lax
from jax.experimental import pallas as pl
from jax.experimental.pallas import tpu as pltpu

f32 = jnp.float32
bf16 = jnp.bfloat16

D_MODEL = 1024
RWKV_HEAD = 64
RWKV_HEADS = 8
RWKV_W = 512
RWKV_LORA = 64
RWKV_PROJ = 3 * RWKV_W + 2 * RWKV_LORA
GLA_HEADS = 4
GLA_DK = 64
GLA_DV = 128
GLA_KW = GLA_HEADS * GLA_DK
GLA_VW = GLA_HEADS * GLA_DV
GLA_LORA = 16
GLA_LORA_PAD = 128
GLA_PROJ = 2 * GLA_KW + GLA_VW + GLA_LORA
EVEN_WIDTH = RWKV_W + GLA_VW
Q_OFF = RWKV_PROJ
K_OFF = Q_OFF + GLA_KW
V_OFF = K_OFF + GLA_KW
GD_OFF = V_OFF + GLA_VW
GATE_OFF = GD_OFF + GLA_LORA_PAD
EVEN_PROJ_PAD = GATE_OFF + EVEN_WIDTH
CONV_C = 1024
CONV_W = 31
CONV_HIST = CONV_W - 1
CONV_HIST_PAD = 32
ODD_PROJ = 3 * CONV_C

RMS_EPS = 1e-6
LN_EPS = 1e-5
RWKV_GN_EPS = 64e-5
GLA_GATE_NORM = 16.0
PROMPT_CHUNK = 64
SAMPLE_PAD_LEN = 8
PROMPT_SEQS_PER_STEP = 4
SAMPLE_SEQS_PER_STEP = 8
CONV_BLOCK = 128
CONV_UNROLL = 2
LANE = 128
SUBLANE = 8
BF16_SUBLANES = 16
HEAD_GROUP = 4
VMEM_LIMIT_BYTES = 48 * 1024 * 1024

NN = ((1,), (0,))
NT = ((1,), (1,))
TN = ((0,), (0,))


def _dg(a, b, dims=NN):
    return lax.dot_general(a, b, (dims, ((), ())), preferred_element_type=f32)


def _dot1(a, b, dims=NN):
    return _dg(a.astype(bf16), b.astype(bf16), dims)


def _split2(x):
    hi = x.astype(bf16)
    lo = (x - hi.astype(f32)).astype(bf16)
    return hi, lo


def _split3(x):
    hi = x.astype(bf16)
    r1 = x - hi.astype(f32)
    mid = r1.astype(bf16)
    lo = (r1 - mid.astype(f32)).astype(bf16)
    return hi, mid, lo


def _dot3(a, b):
    ah, al = _split2(a)
    bh, bl = _split2(b)
    return _dg(ah, bh) + _dg(ah, bl) + _dg(al, bh)


def _dot_exact_lhs(m, x):
    hi, mid, lo = _split3(x)
    return _dg(m, hi) + _dg(m, mid) + _dg(m, lo)


def _sigmoid(x):
    return 0.5 * jnp.tanh(0.5 * x) + 0.5


def _log_sigmoid(x):
    return jnp.minimum(x, 0.0) - jnp.log(1.0 + jnp.exp(-jnp.abs(x)))


def _rmsnorm(x, g):
    return x * lax.rsqrt(jnp.mean(x * x, axis=-1, keepdims=True) + RMS_EPS) * g


def _inproj_kernel(x_ref, g_ref, w_ref, b_ref, o_ref):
    h = _rmsnorm(x_ref[...], g_ref[...])
    o_ref[...] = _dg(h.astype(bf16), w_ref[...]) + b_ref[...]


def _inproj(x2d, g, w, b):
    m, n = x2d.shape[0], w.shape[1]
    tm = 512
    return pl.pallas_call(
        _inproj_kernel,
        grid=(m // tm,),
        in_specs=[
            pl.BlockSpec((tm, D_MODEL), lambda i: (i, 0)),
            pl.BlockSpec((1, D_MODEL), lambda i: (0, 0)),
            pl.BlockSpec((D_MODEL, n), lambda i: (0, 0)),
            pl.BlockSpec((1, n), lambda i: (0, 0)),
        ],
        out_specs=pl.BlockSpec((tm, n), lambda i: (i, 0)),
        out_shape=jax.ShapeDtypeStruct((m, n), f32),
        compiler_params=pltpu.CompilerParams(
            dimension_semantics=("arbitrary",), vmem_limit_bytes=VMEM_LIMIT_BYTES),
        name="inproj",
    )(x2d, g, w, b)


def _outproj_kernel(o_ref, x_ref, w_ref, fn_ref, y_ref, *, final):
    y = x_ref[...] + _dg(o_ref[...].astype(bf16), w_ref[...])
    if final:
        y = _rmsnorm(y, fn_ref[...])
    y_ref[...] = y


def _outproj(o2d, x2d, w, fn, final):
    m = x2d.shape[0]
    tm = 512
    return pl.pallas_call(
        functools.partial(_outproj_kernel, final=final),
        grid=(m // tm,),
        in_specs=[
            pl.BlockSpec((tm, D_MODEL), lambda i: (i, 0)),
            pl.BlockSpec((tm, D_MODEL), lambda i: (i, 0)),
            pl.BlockSpec((D_MODEL, D_MODEL), lambda i: (0, 0)),
            pl.BlockSpec((1, D_MODEL), lambda i: (0, 0)),
        ],
        out_specs=pl.BlockSpec((tm, D_MODEL), lambda i: (i, 0)),
        out_shape=jax.ShapeDtypeStruct((m, D_MODEL), f32),
        compiler_params=pltpu.CompilerParams(
            dimension_semantics=("arbitrary",), vmem_limit_bytes=VMEM_LIMIT_BYTES),
        name="outproj",
    )(o2d, x2d, w, fn)


N_EVEN_WEIGHTS = 16


def _even_mix_kernel(*refs, t, n_valid, nb, has_state, n_alias):
    p_ref = refs[0]
    n_in = 1 + (3 if has_state else 0)
    (mu_ref, w0_ref, wup_ref, a0_ref, aup_ref, kk_ref, ka_ref, rk_ref, lnw_ref, lnb_ref, gup_ref, gb_ref, gn_ref,
     bd_ref, fmask_ref, tmask_ref) = refs[n_in:n_in + N_EVEN_WEIGHTS]
    og_ref, sr_ref, sg_ref, prev_ref, srs_ref = refs[n_in + N_EVEN_WEIGHTS + n_alias:]
    c = pl.program_id(1)
    n_groups = RWKV_HEADS // HEAD_GROUP

    @pl.when(c == 0)
    def _():
        if not has_state:
            sg_ref[...] = jnp.zeros(sg_ref.shape, f32)
            prev_ref[...] = jnp.zeros(prev_ref.shape, f32)
            srs_ref[...] = jnp.zeros(srs_ref.shape, f32)
            return
        shift_ref, sr_in_ref, sg_in_ref = refs[1:4]
        sg_ref[...] = sg_in_ref[...]
        prev_ref[...] = shift_ref[...]
        for i in range(nb):
            for q in range(n_groups):
                rows_ = []
                for j in range(HEAD_GROUP):
                    parts = [jnp.zeros((RWKV_HEAD, RWKV_HEAD), f32)] * HEAD_GROUP
                    parts[j] = sr_in_ref[i, q * HEAD_GROUP + j]
                    rows_.append(jnp.concatenate(parts, axis=1))
                srs_ref[i, q] = jnp.concatenate(rows_, axis=0)

    row = lax.broadcasted_iota(jnp.int32, (t, t), 0)
    col = lax.broadcasted_iota(jnp.int32, (t, t), 1)
    incl = row >= col
    tril = jnp.where(incl, 1.0, 0.0).astype(bf16)
    rowv = lax.broadcasted_iota(jnp.int32, (t, 1), 0)
    valid = None if n_valid == t else rowv < n_valid
    bd = bd_ref[...]
    rk64 = lax.broadcasted_iota(jnp.int32, (GLA_DK, GLA_DK), 0)
    ck64 = lax.broadcasted_iota(jnp.int32, (GLA_DK, GLA_DK), 1)

    def head_sum(x):
        tiles = x.shape[1] // LANE
        stacked = jnp.concatenate([x[:, m * LANE:(m + 1) * LANE] for m in range(tiles)], axis=0)
        hi, lo = _split2(stacked)
        s = _dg(hi, bd) + _dg(lo, bd)
        return jnp.concatenate([s[m * t:(m + 1) * t] for m in range(tiles)], axis=1)

    gpairs = [(i, h) for i in range(nb) for h in range(GLA_HEADS)]

    seq = []
    for i in range(nb):
        pr = p_ref[i, :, 0:RWKV_PROJ]
        p_prev = jnp.where(rowv == 0, prev_ref[i], pltpu.roll(pr, 1, 0))
        prev_ref[i] = pr[t - 1:t, :]
        xm = pr + (p_prev - pr) * mu_ref[...]
        r = xm[:, 0:RWKV_W]
        k = xm[:, RWKV_W:2 * RWKV_W]
        v = xm[:, 2 * RWKV_W:3 * RWKV_W]
        xwa = xm[:, 3 * RWKV_W:RWKV_PROJ]
        wl = w0_ref[...] + _dot3(jnp.tanh(xwa), wup_ref[...])
        al = a0_ref[...] + _dot3(xwa, aup_ref[...])
        ld = (-math.exp(-0.5)) * _sigmoid(wl)
        a = _sigmoid(al)
        kkr = k * kk_ref[...]
        kkn = kkr * lax.rsqrt(jnp.maximum(head_sum(kkr * kkr), 1e-24))
        k2 = k * (1.0 + (a - 1.0) * ka_ref[...])
        if valid is not None:
            ld = jnp.where(valid, ld, 0.0)
            kkn = jnp.where(valid, kkn, 0.0)
            k2 = jnp.where(valid, k2, 0.0)
        c_in = _dot_exact_lhs(tril, ld)
        g_in = jnp.exp(c_in)
        ginv = jnp.exp(-c_in)
        seq.append(dict(r=r, k2=k2, v=v, g_last=g_in[t - 1:t, :],
                        abar=-kkn * jnp.exp(c_in - ld), bbar=kkn * a * ginv, kbar=k2 * ginv, rbar=r * g_in))

    fw = HEAD_GROUP * RWKV_HEAD
    tw = HEAD_GROUP * t
    rt = lax.broadcasted_iota(jnp.int32, (t, tw), 0)
    st = lax.broadcasted_iota(jnp.int32, (t, tw), 1) % t
    strict_g = rt > st
    incl_g = rt >= st
    eye_g = jnp.where(rt == st, 1.0, 0.0).astype(f32)
    first_level_g = (rt // 2 == st // 2) & (rt % 2 == 1) & (st % 2 == 0)
    rs = lax.broadcasted_iota(jnp.int32, (fw, fw), 0) // RWKV_HEAD
    cs = lax.broadcasted_iota(jnp.int32, (fw, fw), 1) // RWKV_HEAD
    state_diag = rs == cs

    def block_diag(x, mask):
        if x.shape[0] % BF16_SUBLANES == 0:
            xb = x.astype(bf16)
            return jnp.concatenate([xb] * HEAD_GROUP, axis=0) * mask
        return (jnp.concatenate([x] * HEAD_GROUP, axis=0) * mask.astype(f32)).astype(bf16)

    fmask = fmask_ref[...]
    groups = [(i, q) for i in range(nb) for q in range(RWKV_HEADS // HEAD_GROUP)]
    fs = lambda q: slice(q * fw, (q + 1) * fw)
    lm = {(i, q): jnp.concatenate([seq[i]["abar"][:, fs(q)], seq[i]["rbar"][:, fs(q)]], axis=0).astype(bf16)
          for i, q in groups}
    vq = {(i, q): seq[i]["v"][:, fs(q)] for i, q in groups}
    vbd = {iq: block_diag(vq[iq], fmask) for iq in groups}
    gb = {(i, q): _dg(lm[i, q], block_diag(seq[i]["bbar"][:, fs(q)], fmask), NT) for i, q in groups}
    gk = {(i, q): _dg(lm[i, q], block_diag(seq[i]["kbar"][:, fs(q)], fmask), NT) for i, q in groups}
    a_ab = {iq: jnp.where(strict_g, gb[iq][:t], 0.0) for iq in groups}
    a_ak = {iq: jnp.where(strict_g, gk[iq][:t], 0.0).astype(bf16) for iq in groups}
    a_rb = {iq: jnp.where(incl_g, gb[iq][t:], 0.0).astype(bf16) for iq in groups}
    a_rk = {iq: jnp.where(incl_g, gk[iq][t:], 0.0).astype(bf16) for iq in groups}
    minv = {iq: eye_g + jnp.where(first_level_g, a_ab[iq], 0.0) for iq in groups}
    for lvl in range(1, tmask_ref.shape[0]):
        tmp = {iq: _dg(minv[iq].astype(bf16), block_diag(a_ab[iq], tmask_ref[lvl])) for iq in groups}
        minv = {iq: minv[iq] + _dg(tmp[iq].astype(bf16), block_diag(minv[iq], tmask_ref[0])) for iq in groups}
    akv = {iq: _dg(a_ak[iq], vbd[iq]) for iq in groups}
    s_old = {(i, q): srs_ref[i, q] for i, q in groups}
    ls = {iq: _dg(lm[iq], s_old[iq].astype(bf16), NT) for iq in groups}
    z = {iq: _dg(minv[iq].astype(bf16), block_diag(ls[iq][:t] + akv[iq], fmask)) for iq in groups}
    yq = {iq: ls[iq][t:] + _dg(a_rb[iq], block_diag(z[iq], fmask)) + _dg(a_rk[iq], vbd[iq]) for iq in groups}
    for i, q in groups:
        zv = jnp.concatenate([z[i, q], vq[i, q]], axis=0).astype(bf16)
        rm = jnp.concatenate([seq[i]["bbar"][:, fs(q)], seq[i]["kbar"][:, fs(q)]], axis=0).astype(bf16)
        upd = jnp.where(state_diag, _dg(zv, rm, TN), 0.0)
        srs_ref[i, q] = (s_old[i, q] + upd) * seq[i]["g_last"][:, fs(q)]

    gseq = []
    for i in range(nb):
        q = p_ref[i, :, Q_OFF:K_OFF]
        kg = p_ref[i, :, K_OFF:V_OFF]
        gd = p_ref[i, :, GD_OFF:GATE_OFF]
        la = _log_sigmoid(_dot3(gd, gup_ref[...]) + gb_ref[...]) * (1.0 / GLA_GATE_NORM)
        if valid is not None:
            la = jnp.where(valid, la, 0.0)
            kg = jnp.where(valid, kg, 0.0)
        b = _dot_exact_lhs(tril, la)
        b_last = b[t - 1:t, :]
        gseq.append(dict(qg=(q * jnp.exp(b) * (GLA_DK ** -0.5)).astype(bf16), kgi=(kg * jnp.exp(-b)).astype(bf16),
                         kd=(kg * jnp.exp(b_last - b)).astype(bf16), eb=jnp.exp(b_last)))
    gs = lambda h: slice(h * GLA_DK, (h + 1) * GLA_DK)
    gv = {(i, h): p_ref[i, :, V_OFF + h * GLA_DV:V_OFF + (h + 1) * GLA_DV].astype(bf16) for i, h in gpairs}
    att = {(i, h): jnp.where(incl, _dg(gseq[i]["qg"][:, gs(h)], gseq[i]["kgi"][:, gs(h)], NT), 0.0).astype(bf16)
           for i, h in gpairs}
    gs_old = {(i, h): sg_ref[i, h] for i, h in gpairs}
    go = {(i, h): _dg(att[i, h], gv[i, h]) + _dg(gseq[i]["qg"][:, gs(h)], gs_old[i, h].astype(bf16))
          for i, h in gpairs}
    for i, h in gpairs:
        e_col = jnp.sum(jnp.where(rk64 == ck64, gseq[i]["eb"][:, gs(h)], 0.0), axis=1, keepdims=True)
        sg_ref[i, h] = e_col * gs_old[i, h] + _dg(gseq[i]["kd"][:, gs(h)], gv[i, h], TN)

    inv_k = 1.0 / RWKV_HEAD
    for i in range(nb):
        y = jnp.concatenate([yq[i, q] for q in range(n_groups)], axis=1)
        mean = head_sum(y) * inv_k
        dlt = y - mean
        var = head_sum(dlt * dlt) * inv_k
        yn = dlt * lax.rsqrt(var + RWKV_GN_EPS) * lnw_ref[...] + lnb_ref[...]
        o_r = yn + head_sum(seq[i]["r"] * seq[i]["k2"] * rk_ref[...]) * seq[i]["v"]
        o_g = jnp.concatenate([_rmsnorm(go[i, h], gn_ref[...]) for h in range(GLA_HEADS)], axis=1)
        gate = p_ref[i, :, GATE_OFF:EVEN_PROJ_PAD]
        og_ref[i] = jnp.concatenate([o_r, o_g], axis=1) * (gate * _sigmoid(gate))

    @pl.when(c == pl.num_programs(1) - 1)
    def _():
        for i in range(nb):
            for q in range(n_groups):
                for j in range(HEAD_GROUP):
                    lo = j * RWKV_HEAD
                    sr_ref[i, q * HEAD_GROUP + j] = srs_ref[i, q, lo:lo + RWKV_HEAD, lo:lo + RWKV_HEAD]


def _group_masks(t):
    rows = np.arange(HEAD_GROUP * t)[:, None]
    fmask = rows // t == np.arange(HEAD_GROUP * RWKV_HEAD)[None, :] // RWKV_HEAD
    lanes = np.arange(HEAD_GROUP * t)[None, :]
    same_head = rows // t == lanes // t
    r, s = rows % t, lanes % t
    levels = [same_head]
    blk = 2
    while blk < t:
        levels.append(same_head & (r // (2 * blk) == s // (2 * blk)) & ((r // blk) % 2 == 1) & ((s // blk) % 2 == 0))
        blk *= 2
    return jnp.asarray(fmask, bf16), jnp.asarray(np.stack(levels), bf16)


def _even_mix(p, layer, n_layers, states, prev_outs, wts, t, n_valid, nb):
    bsz, lp, _ = p.shape
    nc = lp // t
    const = lambda shape: pl.BlockSpec(shape, lambda b, c: (0,) * len(shape))
    fmask, tmask = _group_masks(t)
    n_groups = RWKV_HEADS // HEAD_GROUP
    slab = HEAD_GROUP * RWKV_HEAD
    sr_block = (None, nb, RWKV_HEADS, RWKV_HEAD, RWKV_HEAD)
    sg_block = (None, nb, GLA_HEADS, GLA_DK, GLA_DV)
    layer_map = lambda b, c: (layer, b, 0, 0, 0)
    state_specs, state_args = [], []
    if states is not None:
        shift, sr0, sg0 = states
        state_specs = [pl.BlockSpec((None, nb, 1, RWKV_PROJ), lambda b, c: (layer, b, 0, 0)),
                       pl.BlockSpec(sr_block, layer_map), pl.BlockSpec(sg_block, layer_map)]
        state_args = [shift.reshape(shift.shape[0], bsz, 1, RWKV_PROJ), sr0, sg0]
    alias_specs, alias_args, aliases = [], [], {}
    if prev_outs is not None:
        alias_specs = [pl.BlockSpec(memory_space=pl.ANY)] * 2
        alias_args = list(prev_outs)
        first = 1 + len(state_args) + N_EVEN_WEIGHTS
        aliases = {first: 1, first + 1: 2}
    return pl.pallas_call(
        functools.partial(_even_mix_kernel, t=t, n_valid=n_valid, nb=nb, has_state=states is not None,
                          n_alias=len(alias_args)),
        grid=(bsz // nb, nc),
        in_specs=[
            pl.BlockSpec((nb, t, EVEN_PROJ_PAD), lambda b, c: (b, c, 0)),
            *state_specs,
            const((1, RWKV_PROJ)),
            const((1, RWKV_W)), const((2 * RWKV_LORA, RWKV_W)),
            const((1, RWKV_W)), const((2 * RWKV_LORA, RWKV_W)),
            const((1, RWKV_W)), const((1, RWKV_W)), const((1, RWKV_W)),
            const((1, RWKV_W)), const((1, RWKV_W)),
            const((GLA_LORA_PAD, GLA_KW)), const((1, GLA_KW)), const((1, GLA_DV)),
            const((LANE, LANE)), const(fmask.shape), const(tmask.shape),
            *alias_specs,
        ],
        out_specs=[
            pl.BlockSpec((nb, t, EVEN_WIDTH), lambda b, c: (b, c, 0)),
            pl.BlockSpec(sr_block, layer_map),
            pl.BlockSpec(sg_block, layer_map),
        ],
        out_shape=[
            jax.ShapeDtypeStruct((bsz, lp, EVEN_WIDTH), f32),
            jax.ShapeDtypeStruct((n_layers, bsz, RWKV_HEADS, RWKV_HEAD, RWKV_HEAD), f32),
            jax.ShapeDtypeStruct((n_layers, bsz, GLA_HEADS, GLA_DK, GLA_DV), f32),
        ],
        input_output_aliases=aliases,
        scratch_shapes=[pltpu.VMEM((nb, 1, RWKV_PROJ), f32), pltpu.VMEM((nb, n_groups, slab, slab), f32)],
        compiler_params=pltpu.CompilerParams(
            dimension_semantics=("arbitrary", "arbitrary"), vmem_limit_bytes=VMEM_LIMIT_BYTES),
        name="even_mix",
    )(p, *state_args, *wts, fmask, tmask, *alias_args)


def _odd_mix_kernel(*refs, tb, n_valid, nb, has_cache, n_alias):
    p_ref = refs[0]
    n_in = 2 if has_cache else 1
    cw_ref, cb_ref, lnw_ref, lnb_ref = refs[n_in:n_in + 4]
    o_ref, cache_out_ref, bases_ref, ext_ref, y_ref = refs[n_in + 4 + n_alias:]
    for i in range(nb):
        _odd_mix_one(p_ref.at[i], refs[1].at[i] if has_cache else None, cw_ref, cb_ref, lnw_ref, lnb_ref,
                     o_ref.at[i], cache_out_ref.at[i], bases_ref.at[i], ext_ref, y_ref, tb=tb, n_valid=n_valid)


def _odd_mix_one(p_ref, cache_ref, cw_ref, cb_ref, lnw_ref, lnb_ref, o_ref, cache_out_ref, base_ref, ext_ref,
                 y_ref, *, tb, n_valid):
    c = pl.program_id(1)
    pad = CONV_HIST_PAD - CONV_HIST
    rows = CONV_HIST_PAD + tb

    @pl.when(c == 0)
    def _():
        if cache_ref is not None:
            base_ref[0:SUBLANE, :] = jnp.zeros((SUBLANE, CONV_C), f32)
            base_ref[pad:CONV_HIST_PAD, :] = cache_ref[...]
        else:
            base_ref[0:CONV_HIST_PAD, :] = jnp.zeros((CONV_HIST_PAD, CONV_C), f32)
        base_ref[rows:rows + SUBLANE, :] = jnp.zeros((SUBLANE, CONV_C), f32)

    base_ref[CONV_HIST_PAD:rows, :] = p_ref[:, 0:CONV_C] * _sigmoid(p_ref[:, CONV_C:2 * CONV_C])
    for sh in range(SUBLANE):
        for ct in range(CONV_C // LANE):
            ext_ref[sh, ct] = base_ref[sh:sh + rows, ct * LANE:(ct + 1) * LANE]

    taps = [[j for j in range(CONV_W) if (pad + j) % SUBLANE == sh] for sh in range(SUBLANE)]
    first = [((pad + taps[sh][0]) - sh) // SUBLANE for sh in range(SUBLANE)]
    depth = [((pad + taps[sh][-1]) - sh) // SUBLANE - first[sh] + 1 for sh in range(SUBLANE)]
    for ct in range(CONV_C // LANE):
        ls = slice(ct * LANE, (ct + 1) * LANE)
        bias = jnp.broadcast_to(cb_ref[:, ls], (SUBLANE, LANE))

        def tile(sh, idx, ct=ct):
            return ext_ref[sh, ct, pl.ds(pl.multiple_of(idx * SUBLANE, SUBLANE), SUBLANE), :]

        def body(k, carry, ls=ls, bias=bias, tile=tile):
            new_carry, partial = [], []
            for sh in range(SUBLANE):
                tiles = list(carry[sh]) + [tile(sh, k + first[sh] + depth[sh] - 1)]
                acc = None
                for j in taps[sh]:
                    term = cw_ref[j:j + 1, ls] * tiles[((pad + j) - sh) // SUBLANE - first[sh]]
                    acc = term if acc is None else acc + term
                partial.append(acc)
                new_carry.append(tuple(tiles[1:]))
            while len(partial) > 1:
                partial = [partial[i] + partial[i + 1] for i in range(0, len(partial), 2)]
            y_ref[pl.ds(pl.multiple_of(k * SUBLANE, SUBLANE), SUBLANE), ls] = partial[0] + bias
            return tuple(new_carry)

        init = tuple(tuple(tile(sh, first[sh] + a) for a in range(depth[sh] - 1)) for sh in range(SUBLANE))
        lax.fori_loop(0, tb // SUBLANE, body, init, unroll=min(CONV_UNROLL, tb // SUBLANE))

    y = y_ref[...]
    mu = jnp.mean(y, axis=-1, keepdims=True)
    d = y - mu
    var = jnp.mean(d * d, axis=-1, keepdims=True)
    yn = d * lax.rsqrt(var + LN_EPS) * lnw_ref[...] + lnb_ref[...]
    g = p_ref[:, 2 * CONV_C:3 * CONV_C]
    o_ref[...] = (yn * _sigmoid(yn)) * (g * _sigmoid(g))

    @pl.when(c == pl.num_programs(1) - 1)
    def _():
        cache_out_ref[...] = base_ref[pad + n_valid:pad + n_valid + CONV_HIST, :]

    base_ref[0:CONV_HIST_PAD, :] = base_ref[tb:rows, :]


def _odd_mix(p, layer, n_layers, cache, prev_out, cw, cb, lnw, lnb, tb, n_valid, nb):
    bsz, lp, _ = p.shape
    nc = lp // tb
    const = lambda shape: pl.BlockSpec(shape, lambda b, c: (0,) * len(shape))
    cache_block = (None, nb, CONV_HIST, CONV_C)
    layer_map = lambda b, c: (layer, b, 0, 0)
    cache_specs = [] if cache is None else [pl.BlockSpec(cache_block, layer_map)]
    cache_args = [] if cache is None else [cache]
    alias_specs = [] if prev_out is None else [pl.BlockSpec(memory_space=pl.ANY)]
    alias_args = [] if prev_out is None else [prev_out]
    aliases = {} if prev_out is None else {1 + len(cache_args) + 4: 1}
    return pl.pallas_call(
        functools.partial(_odd_mix_kernel, tb=tb, n_valid=n_valid, nb=nb, has_cache=cache is not None,
                          n_alias=len(alias_args)),
        grid=(bsz // nb, nc),
        in_specs=[
            pl.BlockSpec((nb, tb, ODD_PROJ), lambda b, c: (b, c, 0)),
            *cache_specs,
            const((CONV_W, CONV_C)), const((1, CONV_C)), const((1, CONV_C)), const((1, CONV_C)),
            *alias_specs,
        ],
        out_specs=[
            pl.BlockSpec((nb, tb, CONV_C), lambda b, c: (b, c, 0)),
            pl.BlockSpec(cache_block, layer_map),
        ],
        out_shape=[
            jax.ShapeDtypeStruct((bsz, lp, CONV_C), f32),
            jax.ShapeDtypeStruct((n_layers, bsz, CONV_HIST, CONV_C), f32),
        ],
        input_output_aliases=aliases,
        scratch_shapes=[pltpu.VMEM((nb, CONV_HIST_PAD + tb + SUBLANE, CONV_C), f32),
                        pltpu.VMEM((SUBLANE, CONV_C // LANE, CONV_HIST_PAD + tb, LANE), f32),
                        pltpu.VMEM((tb, CONV_C), f32)],
        compiler_params=pltpu.CompilerParams(
            dimension_semantics=("arbitrary", "arbitrary"), vmem_limit_bytes=VMEM_LIMIT_BYTES),
        name="odd_mix",
    )(p, *cache_args, cw, cb, lnw, lnb, *alias_args)


def _row(x):
    return x.reshape(1, -1)


def kernel(x_prompt, x_sample, state_rwkv_shift, state_rwkv, state_gla, cache_conv, norm_even, w_in_even, rwkv_mu, rwkv_w0, rwkv_w_up, rwkv_a0, rwkv_a_up, rwkv_k_k, rwkv_k_a, rwkv_r_k, rwkv_ln_w, rwkv_ln_b, gla_g_up, gla_g_b, gla_norm, w_out_even, norm_odd, w_in_odd, b_in_odd, conv_w, conv_b, conv_ln_w, conv_ln_b, w_out_odd, final_norm):
    bp, lp, _ = x_prompt.shape
    bs, ls, _ = x_sample.shape
    depth = norm_even.shape[0] + norm_odd.shape[0]
    xs_pad = jnp.pad(x_sample, ((0, 0), (0, SAMPLE_PAD_LEN - ls), (0, 0)))
    groups = [
        dict(x=x_prompt.reshape(bp * lp, D_MODEL), b=bp, l=lp, nv=lp, t=PROMPT_CHUNK, tb=CONV_BLOCK, prompt=True,
             nb=PROMPT_SEQS_PER_STEP, cnb=1),
        dict(x=xs_pad.reshape(bs * SAMPLE_PAD_LEN, D_MODEL), b=bs, l=SAMPLE_PAD_LEN, nv=ls, t=SAMPLE_PAD_LEN,
             tb=SAMPLE_PAD_LEN, prompt=False, nb=SAMPLE_SEQS_PER_STEP, cnb=SAMPLE_SEQS_PER_STEP),
    ]
    bd = (jnp.arange(LANE)[:, None] // RWKV_HEAD == jnp.arange(LANE)[None, :] // RWKV_HEAD).astype(bf16)
    zeros_lora = jnp.zeros((RWKV_LORA, RWKV_W), f32)
    outs = [dict(shift=[], states=None, conv=None) for _ in groups]
    n_even, n_odd = norm_even.shape[0], norm_odd.shape[0]

    for layer in range(depth):
        i = layer // 2
        final = layer == depth - 1
        if layer % 2 == 0:
            w = w_in_even[i]
            w_pad = jnp.concatenate(
                [w[:, :RWKV_PROJ + GLA_PROJ], jnp.zeros((D_MODEL, GLA_LORA_PAD - GLA_LORA), f32),
                 w[:, RWKV_PROJ + GLA_PROJ:]], axis=1).astype(bf16)
            b_pad = jnp.zeros((1, EVEN_PROJ_PAD), f32)
            wts = (
                _row(rwkv_mu[i]), _row(rwkv_w0[i]), jnp.concatenate([rwkv_w_up[i], zeros_lora], axis=0),
                _row(rwkv_a0[i]), jnp.concatenate([zeros_lora, rwkv_a_up[i]], axis=0),
                _row(rwkv_k_k[i]), _row(rwkv_k_a[i]), _row(rwkv_r_k[i]), _row(rwkv_ln_w[i]), _row(rwkv_ln_b[i]),
                jnp.concatenate([gla_g_up[i], jnp.zeros((GLA_LORA_PAD - GLA_LORA, GLA_KW), f32)], axis=0),
                _row(gla_g_b[i]), _row(gla_norm[i]), bd,
            )
            w_out = w_out_even[i].astype(bf16)
            for gi, g in enumerate(groups):
                p = _inproj(g["x"], _row(norm_even[i]), w_pad, b_pad).reshape(g["b"], g["l"], EVEN_PROJ_PAD)
                states = None if g["prompt"] else (state_rwkv_shift, state_rwkv, state_gla)
                og, sr, sg = _even_mix(p, i, n_even, states, outs[gi]["states"], wts, g["t"],
                                       min(g["nv"], g["t"]), g["nb"])
                outs[gi]["shift"].append(p[:, g["nv"] - 1, :RWKV_PROJ])
                outs[gi]["states"] = (sr, sg)
                g["x"] = _outproj(og.reshape(-1, EVEN_WIDTH), g["x"], w_out, _row(final_norm), final)
        else:
            w = w_in_odd[i].astype(bf16)
            w_out = w_out_odd[i].astype(bf16)
            for gi, g in enumerate(groups):
                p = _inproj(g["x"], _row(norm_odd[i]), w, _row(b_in_odd[i])).reshape(g["b"], g["l"], ODD_PROJ)
                cache = None if g["prompt"] else cache_conv
                y, outs[gi]["conv"] = _odd_mix(p, i, n_odd, cache, outs[gi]["conv"], conv_w[i], _row(conv_b[i]),
                                               _row(conv_ln_w[i]), _row(conv_ln_b[i]), g["tb"],
                                               min(g["nv"], g["tb"]), g["cnb"])
                g["x"] = _outproj(y.reshape(-1, CONV_C), g["x"], w_out, _row(final_norm), final)

    y_prompt = groups[0]["x"].reshape(bp, lp, D_MODEL)
    y_sample = groups[1]["x"].reshape(bs, SAMPLE_PAD_LEN, D_MODEL)[:, :ls]
    op, os_ = outs
    return (y_prompt, y_sample,
            jnp.stack(op["shift"]), jnp.stack(os_["shift"]),
            op["states"][0], os_["states"][0],
            op["states"][1], os_["states"][1],
            op["conv"], os_["conv"])
```

```python
import functools
import math

import jax
import jax.numpy as jnp
import numpy as np
from jax import lax
from jax.experimental import pallas as pl
from jax.experimental.pallas import tpu as pltpu

f32 = jnp.float32
bf16 = jnp.bfloat16

D_MODEL = 1024
RWKV_HEAD = 64
RWKV_HEADS = 8
RWKV_W = 512
RWKV_LORA = 64
RWKV_PROJ = 3 * RWKV_W + 2 * RWKV_LORA
GLA_HEADS = 4
GLA_DK = 64
GLA_DV = 128
GLA_KW = GLA_HEADS * GLA_DK
GLA_VW = GLA_HEADS * GLA_DV
GLA_LORA = 16
GLA_LORA_PAD = 128
GLA_PROJ = 2 * GLA_KW + GLA_VW + GLA_LORA
EVEN_WIDTH = RWKV_W + GLA_VW
Q_OFF = RWKV_PROJ
K_OFF = Q_OFF + GLA_KW
V_OFF = K_OFF + GLA_KW
GD_OFF = V_OFF + GLA_VW
GATE_OFF = GD_OFF + GLA_LORA_PAD
EVEN_PROJ_PAD = GATE_OFF + EVEN_WIDTH
CONV_C = 1024
CONV_W = 31
CONV_HIST = CONV_W - 1
CONV_HIST_PAD = 32
ODD_PROJ = 3 * CONV_C

RMS_EPS = 1e-6
LN_EPS = 1e-5
RWKV_GN_EPS = 64e-5
GLA_GATE_NORM = 16.0
PROMPT_CHUNK = 64
SAMPLE_PAD_LEN = 8
PROMPT_SEQS_PER_STEP = 8
SAMPLE_SEQS_PER_STEP = 8
CONV_BLOCK = 128
CONV_UNROLL = 2
LANE = 128
SUBLANE = 8
BF16_SUBLANES = 16
HEAD_GROUP = 4
VMEM_LIMIT_BYTES = 48 * 1024 * 1024

NN = ((1,), (0,))
NT = ((1,), (1,))
TN = ((0,), (0,))


def _dg(a, b, dims=NN):
    return lax.dot_general(a, b, (dims, ((), ())), preferred_element_type=f32)


def _dot1(a, b, dims=NN):
    return _dg(a.astype(bf16), b.astype(bf16), dims)


def _split2(x):
    hi = x.astype(bf16)
    lo = (x - hi.astype(f32)).astype(bf16)
    return hi, lo


def _split3(x):
    hi = x.astype(bf16)
    r1 = x - hi.astype(f32)
    mid = r1.astype(bf16)
    lo = (r1 - mid.astype(f32)).astype(bf16)
    return hi, mid, lo


def _dot3(a, b_ref):
    ah, al = _split2(a)
    bh, bl = b_ref[0], b_ref[1]
    return _dg(ah, bh) + _dg(ah, bl) + _dg(al, bh)


def _split_weight(w):
    hi = w.astype(bf16)
    return jnp.stack([hi, (w - hi.astype(f32)).astype(bf16)])


def _dot_exact_lhs(m, x):
    hi, mid, lo = _split3(x)
    return _dg(m, hi) + _dg(m, mid) + _dg(m, lo)


def _sigmoid(x):
    return 0.5 * jnp.tanh(0.5 * x) + 0.5


def _log_sigmoid(x):
    return jnp.minimum(x, 0.0) - jnp.log(1.0 + jnp.exp(-jnp.abs(x)))


def _rmsnorm(x, g):
    return x * lax.rsqrt(jnp.mean(x * x, axis=-1, keepdims=True) + RMS_EPS) * g


def _inproj_kernel(x_ref, g_ref, w_ref, b_ref, o_ref):
    h = _rmsnorm(x_ref[...], g_ref[...])
    o_ref[...] = _dg(h.astype(bf16), w_ref[...]) + b_ref[...]


def _inproj(x2d, g, w, b):
    m, n = x2d.shape[0], w.shape[1]
    tm = 512
    return pl.pallas_call(
        _inproj_kernel,
        grid=(m // tm,),
        in_specs=[
            pl.BlockSpec((tm, D_MODEL), lambda i: (i, 0)),
            pl.BlockSpec((1, D_MODEL), lambda i: (0, 0)),
            pl.BlockSpec((D_MODEL, n), lambda i: (0, 0)),
            pl.BlockSpec((1, n), lambda i: (0, 0)),
        ],
        out_specs=pl.BlockSpec((tm, n), lambda i: (i, 0)),
        out_shape=jax.ShapeDtypeStruct((m, n), f32),
        compiler_params=pltpu.CompilerParams(
            dimension_semantics=("arbitrary",), vmem_limit_bytes=VMEM_LIMIT_BYTES),
        name="inproj",
    )(x2d, g, w, b)


def _outproj_kernel(o_ref, x_ref, w_ref, fn_ref, y_ref):
    y_ref[...] = _rmsnorm(x_ref[...] + _dg(o_ref[...].astype(bf16), w_ref[...]), fn_ref[...])


def _outproj(o2d, x2d, w, fn):
    m = x2d.shape[0]
    tm = 512
    return pl.pallas_call(
        _outproj_kernel,
        grid=(m // tm,),
        in_specs=[
            pl.BlockSpec((tm, D_MODEL), lambda i: (i, 0)),
            pl.BlockSpec((tm, D_MODEL), lambda i: (i, 0)),
            pl.BlockSpec((D_MODEL, D_MODEL), lambda i: (0, 0)),
            pl.BlockSpec((1, D_MODEL), lambda i: (0, 0)),
        ],
        out_specs=pl.BlockSpec((tm, D_MODEL), lambda i: (i, 0)),
        out_shape=jax.ShapeDtypeStruct((m, D_MODEL), f32),
        compiler_params=pltpu.CompilerParams(
            dimension_semantics=("arbitrary",), vmem_limit_bytes=VMEM_LIMIT_BYTES),
        name="outproj",
    )(o2d, x2d, w, fn)


def _midproj_kernel(o_ref, x_ref, wo_ref, g_ref, wi_ref, b_ref, xn_ref, p_ref):
    y = x_ref[...] + _dg(o_ref[...].astype(bf16), wo_ref[...])
    xn_ref[...] = y
    p_ref[...] = _dg(_rmsnorm(y, g_ref[...]).astype(bf16), wi_ref[...]) + b_ref[...]


def _midproj(o2d, x2d, w_out, g, w_in, b):
    m, n = x2d.shape[0], w_in.shape[1]
    tm = 512
    row_block = pl.BlockSpec((tm, D_MODEL), lambda i: (i, 0))
    resident = lambda shape: pl.BlockSpec(shape, lambda i: (0, 0), pipeline_mode=pl.Buffered(1))
    return pl.pallas_call(
        _midproj_kernel,
        grid=(m // tm,),
        in_specs=[
            row_block, row_block,
            resident((D_MODEL, D_MODEL)), resident((1, D_MODEL)), resident((D_MODEL, n)), resident((1, n)),
        ],
        out_specs=[row_block, pl.BlockSpec((tm, n), lambda i: (i, 0))],
        out_shape=[jax.ShapeDtypeStruct((m, D_MODEL), f32), jax.ShapeDtypeStruct((m, n), f32)],
        compiler_params=pltpu.CompilerParams(
            dimension_semantics=("arbitrary",), vmem_limit_bytes=VMEM_LIMIT_BYTES),
        name="midproj",
    )(o2d, x2d, w_out, g, w_in, b)


N_EVEN_WEIGHTS = 16


def _even_mix_kernel(*refs, t, n_valid, nb, has_state, n_alias):
    p_ref = refs[0]
    n_in = 1 + (3 if has_state else 0)
    (mu_ref, w0_ref, wup_ref, a0_ref, aup_ref, kk_ref, ka_ref, rk_ref, lnw_ref, lnb_ref, gup_ref, gb_ref, gn_ref,
     bd_ref, fmask_ref, tmask_ref) = refs[n_in:n_in + N_EVEN_WEIGHTS]
    og_ref, sr_ref, sg_ref, prev_ref, srs_ref = refs[n_in + N_EVEN_WEIGHTS + n_alias:]
    c = pl.program_id(1)
    n_groups = RWKV_HEADS // HEAD_GROUP

    @pl.when(c == 0)
    def _():
        if not has_state:
            sg_ref[...] = jnp.zeros(sg_ref.shape, f32)
            prev_ref[...] = jnp.zeros(prev_ref.shape, f32)
            srs_ref[...] = jnp.zeros(srs_ref.shape, f32)
            return
        shift_ref, sr_in_ref, sg_in_ref = refs[1:4]
        sg_ref[...] = sg_in_ref[...]
        prev_ref[...] = shift_ref[...]
        for i in range(nb):
            for q in range(n_groups):
                rows_ = []
                for j in range(HEAD_GROUP):
                    parts = [jnp.zeros((RWKV_HEAD, RWKV_HEAD), f32)] * HEAD_GROUP
                    parts[j] = sr_in_ref[i, q * HEAD_GROUP + j]
                    rows_.append(jnp.concatenate(parts, axis=1))
                srs_ref[i, q] = jnp.concatenate(rows_, axis=0)

    row = lax.broadcasted_iota(jnp.int32, (t, t), 0)
    col = lax.broadcasted_iota(jnp.int32, (t, t), 1)
    incl = row >= col
    tril = jnp.where(incl, 1.0, 0.0).astype(bf16)
    rowv = lax.broadcasted_iota(jnp.int32, (t, 1), 0)
    valid = None if n_valid == t else rowv < n_valid
    bd = bd_ref[...]
    rk64 = lax.broadcasted_iota(jnp.int32, (GLA_DK, GLA_DK), 0)
    ck64 = lax.broadcasted_iota(jnp.int32, (GLA_DK, GLA_DK), 1)

    def head_sum(x):
        tiles = x.shape[1] // LANE
        stacked = jnp.concatenate([x[:, m * LANE:(m + 1) * LANE] for m in range(tiles)], axis=0)
        hi, lo = _split2(stacked)
        s = _dg(hi, bd) + _dg(lo, bd)
        return jnp.concatenate([s[m * t:(m + 1) * t] for m in range(tiles)], axis=1)

    gpairs = [(i, h) for i in range(nb) for h in range(GLA_HEADS)]

    seq = []
    for i in range(nb):
        pr = p_ref[i, :, 0:RWKV_PROJ]
        p_prev = jnp.where(rowv == 0, prev_ref[i], pltpu.roll(pr, 1, 0))
        prev_ref[i] = pr[t - 1:t, :]
        xm = pr + (p_prev - pr) * mu_ref[...]
        r = xm[:, 0:RWKV_W]
        k = xm[:, RWKV_W:2 * RWKV_W]
        v = xm[:, 2 * RWKV_W:3 * RWKV_W]
        xwa = xm[:, 3 * RWKV_W:RWKV_PROJ]
        wl = w0_ref[...] + _dot3(jnp.tanh(xwa), wup_ref)
        al = a0_ref[...] + _dot3(xwa, aup_ref)
        ld = (-math.exp(-0.5)) * _sigmoid(wl)
        a = _sigmoid(al)
        kkr = k * kk_ref[...]
        kkn = kkr * lax.rsqrt(jnp.maximum(head_sum(kkr * kkr), 1e-24))
        k2 = k * (1.0 + (a - 1.0) * ka_ref[...])
        if valid is not None:
            ld = jnp.where(valid, ld, 0.0)
            kkn = jnp.where(valid, kkn, 0.0)
            k2 = jnp.where(valid, k2, 0.0)
        c_in = _dot_exact_lhs(tril, ld)
        g_in = jnp.exp(c_in)
        ginv = jnp.exp(-c_in)
        seq.append(dict(r=r, k2=k2, v=v, g_last=g_in[t - 1:t, :],
                        abar=-kkn * jnp.exp(c_in - ld), bbar=kkn * a * ginv, kbar=k2 * ginv, rbar=r * g_in))

    fw = HEAD_GROUP * RWKV_HEAD
    tw = HEAD_GROUP * t
    rt = lax.broadcasted_iota(jnp.int32, (t, tw), 0)
    st = lax.broadcasted_iota(jnp.int32, (t, tw), 1) % t
    strict_g = rt > st
    incl_g = rt >= st
    eye_g = jnp.where(rt == st, 1.0, 0.0).astype(f32)
    first_level_g = (rt // 2 == st // 2) & (rt % 2 == 1) & (st % 2 == 0)
    rs = lax.broadcasted_iota(jnp.int32, (fw, fw), 0) // RWKV_HEAD
    cs = lax.broadcasted_iota(jnp.int32, (fw, fw), 1) // RWKV_HEAD
    state_diag = rs == cs

    def block_diag(x, mask):
        if x.shape[0] % BF16_SUBLANES == 0:
            xb = x.astype(bf16)
            return jnp.concatenate([xb] * HEAD_GROUP, axis=0) * mask
        return (jnp.concatenate([x] * HEAD_GROUP, axis=0) * mask.astype(f32)).astype(bf16)

    fmask = fmask_ref[...]
    groups = [(i, q) for i in range(nb) for q in range(RWKV_HEADS // HEAD_GROUP)]
    fs = lambda q: slice(q * fw, (q + 1) * fw)
    lm = {(i, q): jnp.concatenate([seq[i]["abar"][:, fs(q)], seq[i]["rbar"][:, fs(q)]], axis=0).astype(bf16)
          for i, q in groups}
    vq = {(i, q): seq[i]["v"][:, fs(q)] for i, q in groups}
    vbd = {iq: block_diag(vq[iq], fmask) for iq in groups}
    gb = {(i, q): _dg(lm[i, q], block_diag(seq[i]["bbar"][:, fs(q)], fmask), NT) for i, q in groups}
    gk = {(i, q): _dg(lm[i, q], block_diag(seq[i]["kbar"][:, fs(q)], fmask), NT) for i, q in groups}
    a_ab = {iq: jnp.where(strict_g, gb[iq][:t], 0.0) for iq in groups}
    a_ak = {iq: jnp.where(strict_g, gk[iq][:t], 0.0).astype(bf16) for iq in groups}
    a_rb = {iq: jnp.where(incl_g, gb[iq][t:], 0.0).astype(bf16) for iq in groups}
    a_rk = {iq: jnp.where(incl_g, gk[iq][t:], 0.0).astype(bf16) for iq in groups}
    minv = {iq: eye_g + jnp.where(first_level_g, a_ab[iq], 0.0) for iq in groups}
    for lvl in range(1, tmask_ref.shape[0]):
        tmp = {iq: _dg(minv[iq].astype(bf16), block_diag(a_ab[iq], tmask_ref[lvl])) for iq in groups}
        minv = {iq: minv[iq] + _dg(tmp[iq].astype(bf16), block_diag(minv[iq], tmask_ref[0])) for iq in groups}
    akv = {iq: _dg(a_ak[iq], vbd[iq]) for iq in groups}
    s_old = {(i, q): srs_ref[i, q] for i, q in groups}
    ls = {iq: _dg(lm[iq], s_old[iq].astype(bf16), NT) for iq in groups}
    z = {iq: _dg(minv[iq].astype(bf16), block_diag(ls[iq][:t] + akv[iq], fmask)) for iq in groups}
    yq = {iq: ls[iq][t:] + _dg(a_rb[iq], block_diag(z[iq], fmask)) + _dg(a_rk[iq], vbd[iq]) for iq in groups}
    for i, q in groups:
        zv = jnp.concatenate([z[i, q], vq[i, q]], axis=0).astype(bf16)
        rm = jnp.concatenate([seq[i]["bbar"][:, fs(q)], seq[i]["kbar"][:, fs(q)]], axis=0).astype(bf16)
        upd = jnp.where(state_diag, _dg(zv, rm, TN), 0.0)
        srs_ref[i, q] = (s_old[i, q] + upd) * seq[i]["g_last"][:, fs(q)]

    gseq = []
    for i in range(nb):
        q = p_ref[i, :, Q_OFF:K_OFF]
        kg = p_ref[i, :, K_OFF:V_OFF]
        gd = p_ref[i, :, GD_OFF:GATE_OFF]
        la = _log_sigmoid(_dot3(gd, gup_ref) + gb_ref[...]) * (1.0 / GLA_GATE_NORM)
        if valid is not None:
            la = jnp.where(valid, la, 0.0)
            kg = jnp.where(valid, kg, 0.0)
        b = _dot_exact_lhs(tril, la)
        b_last = b[t - 1:t, :]
        gseq.append(dict(qg=(q * jnp.exp(b) * (GLA_DK ** -0.5)).astype(bf16), kgi=(kg * jnp.exp(-b)).astype(bf16),
                         kd=(kg * jnp.exp(b_last - b)).astype(bf16), eb=jnp.exp(b_last)))
    gs = lambda h: slice(h * GLA_DK, (h + 1) * GLA_DK)
    gv = {(i, h): p_ref[i, :, V_OFF + h * GLA_DV:V_OFF + (h + 1) * GLA_DV].astype(bf16) for i, h in gpairs}
    att = {(i, h): jnp.where(incl, _dg(gseq[i]["qg"][:, gs(h)], gseq[i]["kgi"][:, gs(h)], NT), 0.0).astype(bf16)
           for i, h in gpairs}
    gs_old = {(i, h): sg_ref[i, h] for i, h in gpairs}
    go = {(i, h): _dg(att[i, h], gv[i, h]) + _dg(gseq[i]["qg"][:, gs(h)], gs_old[i, h].astype(bf16))
          for i, h in gpairs}
    for i, h in gpairs:
        e_col = jnp.sum(jnp.where(rk64 == ck64, gseq[i]["eb"][:, gs(h)], 0.0), axis=1, keepdims=True)
        sg_ref[i, h] = e_col * gs_old[i, h] + _dg(gseq[i]["kd"][:, gs(h)], gv[i, h], TN)

    inv_k = 1.0 / RWKV_HEAD
    for i in range(nb):
        y = jnp.concatenate([yq[i, q] for q in range(n_groups)], axis=1)
        mean = head_sum(y) * inv_k
        dlt = y - mean
        var = head_sum(dlt * dlt) * inv_k
        yn = dlt * lax.rsqrt(var + RWKV_GN_EPS) * lnw_ref[...] + lnb_ref[...]
        o_r = yn + head_sum(seq[i]["r"] * seq[i]["k2"] * rk_ref[...]) * seq[i]["v"]
        o_g = jnp.concatenate([_rmsnorm(go[i, h], gn_ref[...]) for h in range(GLA_HEADS)], axis=1)
        gate = p_ref[i, :, GATE_OFF:EVEN_PROJ_PAD]
        og_ref[i] = jnp.concatenate([o_r, o_g], axis=1) * (gate * _sigmoid(gate))

    @pl.when(c == pl.num_programs(1) - 1)
    def _():
        for i in range(nb):
            for q in range(n_groups):
                for j in range(HEAD_GROUP):
                    lo = j * RWKV_HEAD
                    sr_ref[i, q * HEAD_GROUP + j] = srs_ref[i, q, lo:lo + RWKV_HEAD, lo:lo + RWKV_HEAD]


def _group_masks(t):
    rows = np.arange(HEAD_GROUP * t)[:, None]
    fmask = rows // t == np.arange(HEAD_GROUP * RWKV_HEAD)[None, :] // RWKV_HEAD
    lanes = np.arange(HEAD_GROUP * t)[None, :]
    same_head = rows // t == lanes // t
    r, s = rows % t, lanes % t
    levels = [same_head]
    blk = 2
    while blk < t:
        levels.append(same_head & (r // (2 * blk) == s // (2 * blk)) & ((r // blk) % 2 == 1) & ((s // blk) % 2 == 0))
        blk *= 2
    return jnp.asarray(fmask, bf16), jnp.asarray(np.stack(levels), bf16)


def _even_mix(p, layer, n_layers, states, prev_outs, wts, t, n_valid, nb):
    bsz, lp, _ = p.shape
    nc = lp // t
    const = lambda shape: pl.BlockSpec(shape, lambda b, c: (0,) * len(shape))
    fmask, tmask = _group_masks(t)
    n_groups = RWKV_HEADS // HEAD_GROUP
    slab = HEAD_GROUP * RWKV_HEAD
    sr_block = (None, nb, RWKV_HEADS, RWKV_HEAD, RWKV_HEAD)
    sg_block = (None, nb, GLA_HEADS, GLA_DK, GLA_DV)
    layer_map = lambda b, c: (layer, b, 0, 0, 0)
    state_specs, state_args = [], []
    if states is not None:
        shift, sr0, sg0 = states
        state_specs = [pl.BlockSpec((None, nb, 1, RWKV_PROJ), lambda b, c: (layer, b, 0, 0)),
                       pl.BlockSpec(sr_block, layer_map), pl.BlockSpec(sg_block, layer_map)]
        state_args = [shift.reshape(shift.shape[0], bsz, 1, RWKV_PROJ), sr0, sg0]
    alias_specs, alias_args, aliases = [], [], {}
    if prev_outs is not None:
        alias_specs = [pl.BlockSpec(memory_space=pl.ANY)] * 2
        alias_args = list(prev_outs)
        first = 1 + len(state_args) + N_EVEN_WEIGHTS
        aliases = {first: 1, first + 1: 2}
    return pl.pallas_call(
        functools.partial(_even_mix_kernel, t=t, n_valid=n_valid, nb=nb, has_state=states is not None,
                          n_alias=len(alias_args)),
        grid=(bsz // nb, nc),
        in_specs=[
            pl.BlockSpec((nb, t, EVEN_PROJ_PAD), lambda b, c: (b, c, 0)),
            *state_specs,
            const((1, RWKV_PROJ)),
            const((1, RWKV_W)), const((2, 2 * RWKV_LORA, RWKV_W)),
            const((1, RWKV_W)), const((2, 2 * RWKV_LORA, RWKV_W)),
            const((1, RWKV_W)), const((1, RWKV_W)), const((1, RWKV_W)),
            const((1, RWKV_W)), const((1, RWKV_W)),
            const((2, GLA_LORA_PAD, GLA_KW)), const((1, GLA_KW)), const((1, GLA_DV)),
            const((LANE, LANE)), const(fmask.shape), const(tmask.shape),
            *alias_specs,
        ],
        out_specs=[
            pl.BlockSpec((nb, t, EVEN_WIDTH), lambda b, c: (b, c, 0)),
            pl.BlockSpec(sr_block, layer_map),
            pl.BlockSpec(sg_block, layer_map),
        ],
        out_shape=[
            jax.ShapeDtypeStruct((bsz, lp, EVEN_WIDTH), f32),
            jax.ShapeDtypeStruct((n_layers, bsz, RWKV_HEADS, RWKV_HEAD, RWKV_HEAD), f32),
            jax.ShapeDtypeStruct((n_layers, bsz, GLA_HEADS, GLA_DK, GLA_DV), f32),
        ],
        input_output_aliases=aliases,
        scratch_shapes=[pltpu.VMEM((nb, 1, RWKV_PROJ), f32), pltpu.VMEM((nb, n_groups, slab, slab), f32)],
        compiler_params=pltpu.CompilerParams(
            dimension_semantics=("arbitrary", "arbitrary"), vmem_limit_bytes=VMEM_LIMIT_BYTES),
        name="even_mix",
    )(p, *state_args, *wts, fmask, tmask, *alias_args)


def _odd_mix_kernel(*refs, tb, n_valid, nb, has_cache, n_alias):
    p_ref = refs[0]
    n_in = 2 if has_cache else 1
    cw_ref, cb_ref, lnw_ref, lnb_ref = refs[n_in:n_in + 4]
    o_ref, cache_out_ref, bases_ref, ext_ref, y_ref = refs[n_in + 4 + n_alias:]
    for i in range(nb):
        _odd_mix_one(p_ref.at[i], refs[1].at[i] if has_cache else None, cw_ref, cb_ref, lnw_ref, lnb_ref,
                     o_ref.at[i], cache_out_ref.at[i], bases_ref.at[i], ext_ref, y_ref, tb=tb, n_valid=n_valid)


def _odd_mix_one(p_ref, cache_ref, cw_ref, cb_ref, lnw_ref, lnb_ref, o_ref, cache_out_ref, base_ref, ext_ref,
                 y_ref, *, tb, n_valid):
    c = pl.program_id(1)
    pad = CONV_HIST_PAD - CONV_HIST
    rows = CONV_HIST_PAD + tb

    @pl.when(c == 0)
    def _():
        if cache_ref is not None:
            base_ref[0:SUBLANE, :] = jnp.zeros((SUBLANE, CONV_C), f32)
            base_ref[pad:CONV_HIST_PAD, :] = cache_ref[...]
        else:
            base_ref[0:CONV_HIST_PAD, :] = jnp.zeros((CONV_HIST_PAD, CONV_C), f32)
        base_ref[rows:rows + SUBLANE, :] = jnp.zeros((SUBLANE, CONV_C), f32)

    base_ref[CONV_HIST_PAD:rows, :] = p_ref[:, 0:CONV_C] * _sigmoid(p_ref[:, CONV_C:2 * CONV_C])
    for sh in range(SUBLANE):
        for ct in range(CONV_C // LANE):
            ext_ref[sh, ct] = base_ref[sh:sh + rows, ct * LANE:(ct + 1) * LANE]

    taps = [[j for j in range(CONV_W) if (pad + j) % SUBLANE == sh] for sh in range(SUBLANE)]
    first = [((pad + taps[sh][0]) - sh) // SUBLANE for sh in range(SUBLANE)]
    depth = [((pad + taps[sh][-1]) - sh) // SUBLANE - first[sh] + 1 for sh in range(SUBLANE)]
    for ct in range(CONV_C // LANE):
        ls = slice(ct * LANE, (ct + 1) * LANE)
        bias = jnp.broadcast_to(cb_ref[:, ls], (SUBLANE, LANE))

        def tile(sh, idx, ct=ct):
            return ext_ref[sh, ct, pl.ds(pl.multiple_of(idx * SUBLANE, SUBLANE), SUBLANE), :]

        def body(k, carry, ls=ls, bias=bias, tile=tile):
            new_carry, partial = [], []
            for sh in range(SUBLANE):
                tiles = list(carry[sh]) + [tile(sh, k + first[sh] + depth[sh] - 1)]
                acc = None
                for j in taps[sh]:
                    term = cw_ref[j:j + 1, ls] * tiles[((pad + j) - sh) // SUBLANE - first[sh]]
                    acc = term if acc is None else acc + term
                partial.append(acc)
                new_carry.append(tuple(tiles[1:]))
            while len(partial) > 1:
                partial = [partial[i] + partial[i + 1] for i in range(0, len(partial), 2)]
            y_ref[pl.ds(pl.multiple_of(k * SUBLANE, SUBLANE), SUBLANE), ls] = partial[0] + bias
            return tuple(new_carry)

        init = tuple(tuple(tile(sh, first[sh] + a) for a in range(depth[sh] - 1)) for sh in range(SUBLANE))
        lax.fori_loop(0, tb // SUBLANE, body, init, unroll=min(CONV_UNROLL, tb // SUBLANE))

    y = y_ref[...]
    mu = jnp.mean(y, axis=-1, keepdims=True)
    d = y - mu
    var = jnp.mean(d * d, axis=-1, keepdims=True)
    yn = d * lax.rsqrt(var + LN_EPS) * lnw_ref[...] + lnb_ref[...]
    g = p_ref[:, 2 * CONV_C:3 * CONV_C]
    o_ref[...] = (yn * _sigmoid(yn)) * (g * _sigmoid(g))

    @pl.when(c == pl.num_programs(1) - 1)
    def _():
        cache_out_ref[...] = base_ref[pad + n_valid:pad + n_valid + CONV_HIST, :]

    base_ref[0:CONV_HIST_PAD, :] = base_ref[tb:rows, :]


def _odd_mix(p, layer, n_layers, cache, prev_out, cw, cb, lnw, lnb, tb, n_valid, nb):
    bsz, lp, _ = p.shape
    nc = lp // tb
    const = lambda shape: pl.BlockSpec(shape, lambda b, c: (0,) * len(shape))
    cache_block = (None, nb, CONV_HIST, CONV_C)
    layer_map = lambda b, c: (layer, b, 0, 0)
    cache_specs = [] if cache is None else [pl.BlockSpec(cache_block, layer_map)]
    cache_args = [] if cache is None else [cache]
    alias_specs = [] if prev_out is None else [pl.BlockSpec(memory_space=pl.ANY)]
    alias_args = [] if prev_out is None else [prev_out]
    aliases = {} if prev_out is None else {1 + len(cache_args) + 4: 1}
    return pl.pallas_call(
        functools.partial(_odd_mix_kernel, tb=tb, n_valid=n_valid, nb=nb, has_cache=cache is not None,
                          n_alias=len(alias_args)),
        grid=(bsz // nb, nc),
        in_specs=[
            pl.BlockSpec((nb, tb, ODD_PROJ), lambda b, c: (b, c, 0)),
            *cache_specs,
            const((CONV_W, CONV_C)), const((1, CONV_C)), const((1, CONV_C)), const((1, CONV_C)),
            *alias_specs,
        ],
        out_specs=[
            pl.BlockSpec((nb, tb, CONV_C), lambda b, c: (b, c, 0)),
            pl.BlockSpec(cache_block, layer_map),
        ],
        out_shape=[
            jax.ShapeDtypeStruct((bsz, lp, CONV_C), f32),
            jax.ShapeDtypeStruct((n_layers, bsz, CONV_HIST, CONV_C), f32),
        ],
        input_output_aliases=aliases,
        scratch_shapes=[pltpu.VMEM((nb, CONV_HIST_PAD + tb + SUBLANE, CONV_C), f32),
                        pltpu.VMEM((SUBLANE, CONV_C // LANE, CONV_HIST_PAD + tb, LANE), f32),
                        pltpu.VMEM((tb, CONV_C), f32)],
        compiler_params=pltpu.CompilerParams(
            dimension_semantics=("arbitrary", "arbitrary"), vmem_limit_bytes=VMEM_LIMIT_BYTES),
        name="odd_mix",
    )(p, *cache_args, cw, cb, lnw, lnb, *alias_args)


def _row(x):
    return x.reshape(1, -1)


def kernel(x_prompt, x_sample, state_rwkv_shift, state_rwkv, state_gla, cache_conv, norm_even, w_in_even, rwkv_mu, rwkv_w0, rwkv_w_up, rwkv_a0, rwkv_a_up, rwkv_k_k, rwkv_k_a, rwkv_r_k, rwkv_ln_w, rwkv_ln_b, gla_g_up, gla_g_b, gla_norm, w_out_even, norm_odd, w_in_odd, b_in_odd, conv_w, conv_b, conv_ln_w, conv_ln_b, w_out_odd, final_norm):
    bp, lp, _ = x_prompt.shape
    bs, ls, _ = x_sample.shape
    depth = norm_even.shape[0] + norm_odd.shape[0]
    xs_pad = jnp.pad(x_sample, ((0, 0), (0, SAMPLE_PAD_LEN - ls), (0, 0)))
    groups = [
        dict(x=x_prompt.reshape(bp * lp, D_MODEL), b=bp, l=lp, nv=lp, t=PROMPT_CHUNK, tb=CONV_BLOCK, prompt=True,
             nb=PROMPT_SEQS_PER_STEP, cnb=1),
        dict(x=xs_pad.reshape(bs * SAMPLE_PAD_LEN, D_MODEL), b=bs, l=SAMPLE_PAD_LEN, nv=ls, t=SAMPLE_PAD_LEN,
             tb=SAMPLE_PAD_LEN, prompt=False, nb=SAMPLE_SEQS_PER_STEP, cnb=SAMPLE_SEQS_PER_STEP),
    ]
    bd = (jnp.arange(LANE)[:, None] // RWKV_HEAD == jnp.arange(LANE)[None, :] // RWKV_HEAD).astype(bf16)
    zeros_lora = jnp.zeros((RWKV_LORA, RWKV_W), f32)
    outs = [dict(shift=[], states=None, conv=None) for _ in groups]
    n_even, n_odd = norm_even.shape[0], norm_odd.shape[0]

    proj = []
    for layer in range(depth):
        i = layer // 2
        if layer % 2 == 0:
            w = w_in_even[i]
            w_pad = jnp.concatenate(
                [w[:, :RWKV_PROJ + GLA_PROJ], jnp.zeros((D_MODEL, GLA_LORA_PAD - GLA_LORA), f32),
                 w[:, RWKV_PROJ + GLA_PROJ:]], axis=1).astype(bf16)
            proj.append((_row(norm_even[i]), w_pad, jnp.zeros((1, EVEN_PROJ_PAD), f32), w_out_even[i].astype(bf16)))
        else:
            proj.append((_row(norm_odd[i]), w_in_odd[i].astype(bf16), _row(b_in_odd[i]), w_out_odd[i].astype(bf16)))
    for g in groups:
        g["p"] = _inproj(g["x"], *proj[0][:3])

    def next_layer(g, layer, o2d):
        if layer == depth - 1:
            g["x"] = _outproj(o2d, g["x"], proj[layer][3], _row(final_norm))
        else:
            g["x"], g["p"] = _midproj(o2d, g["x"], proj[layer][3], *proj[layer + 1][:3])

    for layer in range(depth):
        i = layer // 2
        if layer % 2 == 0:
            wts = (
                _row(rwkv_mu[i]), _row(rwkv_w0[i]),
                _split_weight(jnp.concatenate([rwkv_w_up[i], zeros_lora], axis=0)),
                _row(rwkv_a0[i]), _split_weight(jnp.concatenate([zeros_lora, rwkv_a_up[i]], axis=0)),
                _row(rwkv_k_k[i]), _row(rwkv_k_a[i]), _row(rwkv_r_k[i]), _row(rwkv_ln_w[i]), _row(rwkv_ln_b[i]),
                _split_weight(jnp.concatenate([gla_g_up[i], jnp.zeros((GLA_LORA_PAD - GLA_LORA, GLA_KW), f32)],
                                              axis=0)),
                _row(gla_g_b[i]), _row(gla_norm[i]), bd,
            )
            for gi, g in enumerate(groups):
                p = g["p"].reshape(g["b"], g["l"], EVEN_PROJ_PAD)
                states = None if g["prompt"] else (state_rwkv_shift, state_rwkv, state_gla)
                og, sr, sg = _even_mix(p, i, n_even, states, outs[gi]["states"], wts, g["t"],
                                       min(g["nv"], g["t"]), g["nb"])
                outs[gi]["shift"].append(p[:, g["nv"] - 1, :RWKV_PROJ])
                outs[gi]["states"] = (sr, sg)
                next_layer(g, layer, og.reshape(-1, EVEN_WIDTH))
        else:
            for gi, g in enumerate(groups):
                p = g["p"].reshape(g["b"], g["l"], ODD_PROJ)
                cache = None if g["prompt"] else cache_conv
                y, outs[gi]["conv"] = _odd_mix(p, i, n_odd, cache, outs[gi]["conv"], conv_w[i], _row(conv_b[i]),
                                               _row(conv_ln_w[i]), _row(conv_ln_b[i]), g["tb"],
                                               min(g["nv"], g["tb"]), g["cnb"])
                next_layer(g, layer, y.reshape(-1, CONV_C))

    y_prompt = groups[0]["x"].reshape(bp, lp, D_MODEL)
    y_sample = groups[1]["x"].reshape(bs, SAMPLE_PAD_LEN, D_MODEL)[:, :ls]
    op, os_ = outs
    return (y_prompt, y_sample,
            jnp.stack(op["shift"]), jnp.stack(os_["shift"]),
            op["states"][0], os_["states"][0],
            op["states"][1], os_["states"][1],
            op["conv"], os_["conv"])
```

```python
import functools
import math

import jax
import jax.numpy as jnp
import numpy as np
from jax import lax
from jax.experimental import pallas as pl
from jax.experimental.pallas import tpu as pltpu

f32 = jnp.float32
bf16 = jnp.bfloat16

D_MODEL = 1024
RWKV_HEAD = 64
RWKV_HEADS = 8
RWKV_W = 512
RWKV_LORA = 64
RWKV_PROJ = 3 * RWKV_W + 2 * RWKV_LORA
GLA_HEADS = 4
GLA_DK = 64
GLA_DV = 128
GLA_KW = GLA_HEADS * GLA_DK
GLA_VW = GLA_HEADS * GLA_DV
GLA_LORA = 16
GLA_LORA_PAD = 128
GLA_PROJ = 2 * GLA_KW + GLA_VW + GLA_LORA
EVEN_WIDTH = RWKV_W + GLA_VW
Q_OFF = RWKV_PROJ
K_OFF = Q_OFF + GLA_KW
V_OFF = K_OFF + GLA_KW
GD_OFF = V_OFF + GLA_VW
GATE_OFF = GD_OFF + GLA_LORA_PAD
EVEN_PROJ_PAD = GATE_OFF + EVEN_WIDTH
CONV_C = 1024
CONV_W = 31
CONV_HIST = CONV_W - 1
CONV_HIST_PAD = 32
ODD_PROJ = 3 * CONV_C

RMS_EPS = 1e-6
LN_EPS = 1e-5
RWKV_GN_EPS = 64e-5
GLA_GATE_NORM = 16.0
PROMPT_CHUNK = 64
SAMPLE_PAD_LEN = 8
PROMPT_SEQS_PER_STEP = 8
SAMPLE_SEQS_PER_STEP = 8
CONV_BLOCK = 256
CONV_UNROLL = 2
LANE = 128
SUBLANE = 8
BF16_SUBLANES = 16
HEAD_GROUP = 4
VMEM_LIMIT_BYTES = 48 * 1024 * 1024

NN = ((1,), (0,))
NT = ((1,), (1,))
TN = ((0,), (0,))


def _dg(a, b, dims=NN):
    return lax.dot_general(a, b, (dims, ((), ())), preferred_element_type=f32)


def _dot1(a, b, dims=NN):
    return _dg(a.astype(bf16), b.astype(bf16), dims)


def _split2(x):
    hi = x.astype(bf16)
    lo = (x - hi.astype(f32)).astype(bf16)
    return hi, lo


def _split3(x):
    hi = x.astype(bf16)
    r1 = x - hi.astype(f32)
    mid = r1.astype(bf16)
    lo = (r1 - mid.astype(f32)).astype(bf16)
    return hi, mid, lo


def _dot3(a, b_ref):
    ah, al = _split2(a)
    bh, bl = b_ref[0], b_ref[1]
    return _dg(ah, bh) + _dg(ah, bl) + _dg(al, bh)


def _split_weight(w):
    hi = w.astype(bf16)
    return jnp.stack([hi, (w - hi.astype(f32)).astype(bf16)])


def _dot_exact_lhs(m, x):
    hi, mid, lo = _split3(x)
    return _dg(m, hi) + _dg(m, mid) + _dg(m, lo)


def _sigmoid(x):
    return 0.5 * jnp.tanh(0.5 * x) + 0.5


def _log_sigmoid(x):
    return jnp.minimum(x, 0.0) - jnp.log(1.0 + jnp.exp(-jnp.abs(x)))


def _rmsnorm(x, g):
    return x * lax.rsqrt(jnp.mean(x * x, axis=-1, keepdims=True) + RMS_EPS) * g


def _inproj_kernel(x_ref, g_ref, w_ref, b_ref, o_ref):
    h = _rmsnorm(x_ref[...], g_ref[...])
    o_ref[...] = _dg(h.astype(bf16), w_ref[...]) + b_ref[...]


def _inproj(x2d, g, w, b):
    m, n = x2d.shape[0], w.shape[1]
    tm = 512
    return pl.pallas_call(
        _inproj_kernel,
        grid=(m // tm,),
        in_specs=[
            pl.BlockSpec((tm, D_MODEL), lambda i: (i, 0)),
            pl.BlockSpec((1, D_MODEL), lambda i: (0, 0)),
            pl.BlockSpec((D_MODEL, n), lambda i: (0, 0)),
            pl.BlockSpec((1, n), lambda i: (0, 0)),
        ],
        out_specs=pl.BlockSpec((tm, n), lambda i: (i, 0)),
        out_shape=jax.ShapeDtypeStruct((m, n), f32),
        compiler_params=pltpu.CompilerParams(
            dimension_semantics=("arbitrary",), vmem_limit_bytes=VMEM_LIMIT_BYTES),
        name="inproj",
    )(x2d, g, w, b)


def _outproj_kernel(o_ref, x_ref, w_ref, fn_ref, y_ref):
    y_ref[...] = _rmsnorm(x_ref[...] + _dg(o_ref[...].astype(bf16), w_ref[...]), fn_ref[...])


def _outproj(o2d, x2d, w, fn):
    m = x2d.shape[0]
    tm = 512
    return pl.pallas_call(
        _outproj_kernel,
        grid=(m // tm,),
        in_specs=[
            pl.BlockSpec((tm, D_MODEL), lambda i: (i, 0)),
            pl.BlockSpec((tm, D_MODEL), lambda i: (i, 0)),
            pl.BlockSpec((D_MODEL, D_MODEL), lambda i: (0, 0)),
            pl.BlockSpec((1, D_MODEL), lambda i: (0, 0)),
        ],
        out_specs=pl.BlockSpec((tm, D_MODEL), lambda i: (i, 0)),
        out_shape=jax.ShapeDtypeStruct((m, D_MODEL), f32),
        compiler_params=pltpu.CompilerParams(
            dimension_semantics=("arbitrary",), vmem_limit_bytes=VMEM_LIMIT_BYTES),
        name="outproj",
    )(o2d, x2d, w, fn)


def _midproj_kernel(o_ref, x_ref, wo_ref, g_ref, wi_ref, b_ref, xn_ref, p_ref):
    y = x_ref[...] + _dg(o_ref[...].astype(bf16), wo_ref[...])
    xn_ref[...] = y
    p_ref[...] = _dg(_rmsnorm(y, g_ref[...]).astype(bf16), wi_ref[...]) + b_ref[...]


def _midproj(o2d, x2d, w_out, g, w_in, b):
    m, n = x2d.shape[0], w_in.shape[1]
    tm = 512
    row_block = pl.BlockSpec((tm, D_MODEL), lambda i: (i, 0))
    resident = lambda shape: pl.BlockSpec(shape, lambda i: (0, 0), pipeline_mode=pl.Buffered(1))
    return pl.pallas_call(
        _midproj_kernel,
        grid=(m // tm,),
        in_specs=[
            row_block, row_block,
            resident((D_MODEL, D_MODEL)), resident((1, D_MODEL)), resident((D_MODEL, n)), resident((1, n)),
        ],
        out_specs=[row_block, pl.BlockSpec((tm, n), lambda i: (i, 0))],
        out_shape=[jax.ShapeDtypeStruct((m, D_MODEL), f32), jax.ShapeDtypeStruct((m, n), f32)],
        compiler_params=pltpu.CompilerParams(
            dimension_semantics=("arbitrary",), vmem_limit_bytes=VMEM_LIMIT_BYTES),
        name="midproj",
    )(o2d, x2d, w_out, g, w_in, b)


N_EVEN_WEIGHTS = 16


def _even_mix_kernel(*refs, t, n_valid, nb, has_state, n_alias):
    p_ref = refs[0]
    n_in = 1 + (3 if has_state else 0)
    (mu_ref, w0_ref, wup_ref, a0_ref, aup_ref, kk_ref, ka_ref, rk_ref, lnw_ref, lnb_ref, gup_ref, gb_ref, gn_ref,
     bd_ref, fmask_ref, tmask_ref) = refs[n_in:n_in + N_EVEN_WEIGHTS]
    og_ref, sr_ref, sg_ref, prev_ref, srs_ref = refs[n_in + N_EVEN_WEIGHTS + n_alias:]
    c = pl.program_id(1)
    n_groups = RWKV_HEADS // HEAD_GROUP

    @pl.when(c == 0)
    def _():
        if not has_state:
            sg_ref[...] = jnp.zeros(sg_ref.shape, f32)
            prev_ref[...] = jnp.zeros(prev_ref.shape, f32)
            srs_ref[...] = jnp.zeros(srs_ref.shape, f32)
            return
        shift_ref, sr_in_ref, sg_in_ref = refs[1:4]
        sg_ref[...] = sg_in_ref[...]
        prev_ref[...] = shift_ref[...]
        for i in range(nb):
            for q in range(n_groups):
                rows_ = []
                for j in range(HEAD_GROUP):
                    parts = [jnp.zeros((RWKV_HEAD, RWKV_HEAD), f32)] * HEAD_GROUP
                    parts[j] = sr_in_ref[i, q * HEAD_GROUP + j]
                    rows_.append(jnp.concatenate(parts, axis=1))
                srs_ref[i, q] = jnp.concatenate(rows_, axis=0)

    row = lax.broadcasted_iota(jnp.int32, (t, t), 0)
    col = lax.broadcasted_iota(jnp.int32, (t, t), 1)
    incl = row >= col
    tril = jnp.where(incl, 1.0, 0.0).astype(bf16)
    rowv = lax.broadcasted_iota(jnp.int32, (t, 1), 0)
    valid = None if n_valid == t else rowv < n_valid
    bd = bd_ref[...]
    rk64 = lax.broadcasted_iota(jnp.int32, (GLA_DK, GLA_DK), 0)
    ck64 = lax.broadcasted_iota(jnp.int32, (GLA_DK, GLA_DK), 1)

    def head_sum(x):
        tiles = x.shape[1] // LANE
        stacked = jnp.concatenate([x[:, m * LANE:(m + 1) * LANE] for m in range(tiles)], axis=0)
        hi, lo = _split2(stacked)
        s = _dg(hi, bd) + _dg(lo, bd)
        return jnp.concatenate([s[m * t:(m + 1) * t] for m in range(tiles)], axis=1)

    gpairs = [(i, h) for i in range(nb) for h in range(GLA_HEADS)]

    seq = []
    for i in range(nb):
        pr = p_ref[i, :, 0:RWKV_PROJ]
        p_prev = jnp.where(rowv == 0, prev_ref[i], pltpu.roll(pr, 1, 0))
        prev_ref[i] = pr[t - 1:t, :]
        xm = pr + (p_prev - pr) * mu_ref[...]
        r = xm[:, 0:RWKV_W]
        k = xm[:, RWKV_W:2 * RWKV_W]
        v = xm[:, 2 * RWKV_W:3 * RWKV_W]
        xwa = xm[:, 3 * RWKV_W:RWKV_PROJ]
        wl = w0_ref[...] + _dot3(jnp.tanh(xwa), wup_ref)
        al = a0_ref[...] + _dot3(xwa, aup_ref)
        ld = (-math.exp(-0.5)) * _sigmoid(wl)
        a = _sigmoid(al)
        kkr = k * kk_ref[...]
        kkn = kkr * lax.rsqrt(jnp.maximum(head_sum(kkr * kkr), 1e-24))
        k2 = k * (1.0 + (a - 1.0) * ka_ref[...])
        if valid is not None:
            ld = jnp.where(valid, ld, 0.0)
            kkn = jnp.where(valid, kkn, 0.0)
            k2 = jnp.where(valid, k2, 0.0)
        c_in = _dot_exact_lhs(tril, ld)
        g_in = jnp.exp(c_in)
        ginv = jnp.exp(-c_in)
        seq.append(dict(r=r, k2=k2, v=v, g_last=g_in[t - 1:t, :],
                        abar=-kkn * jnp.exp(c_in - ld), bbar=kkn * a * ginv, kbar=k2 * ginv, rbar=r * g_in))

    fw = HEAD_GROUP * RWKV_HEAD
    tw = HEAD_GROUP * t
    rt = lax.broadcasted_iota(jnp.int32, (t, tw), 0)
    st = lax.broadcasted_iota(jnp.int32, (t, tw), 1) % t
    strict_g = rt > st
    incl_g = rt >= st
    eye_g = jnp.where(rt == st, 1.0, 0.0).astype(f32)
    first_level_g = (rt // 2 == st // 2) & (rt % 2 == 1) & (st % 2 == 0)
    rs = lax.broadcasted_iota(jnp.int32, (fw, fw), 0) // RWKV_HEAD
    cs = lax.broadcasted_iota(jnp.int32, (fw, fw), 1) // RWKV_HEAD
    state_diag = rs == cs

    def block_diag(x, mask):
        if x.shape[0] % BF16_SUBLANES == 0:
            xb = x.astype(bf16)
            return jnp.concatenate([xb] * HEAD_GROUP, axis=0) * mask
        return (jnp.concatenate([x] * HEAD_GROUP, axis=0) * mask.astype(f32)).astype(bf16)

    fmask = fmask_ref[...]
    groups = [(i, q) for i in range(nb) for q in range(RWKV_HEADS // HEAD_GROUP)]
    fs = lambda q: slice(q * fw, (q + 1) * fw)
    lm = {(i, q): jnp.concatenate([seq[i]["abar"][:, fs(q)], seq[i]["rbar"][:, fs(q)]], axis=0).astype(bf16)
          for i, q in groups}
    vq = {(i, q): seq[i]["v"][:, fs(q)] for i, q in groups}
    vbd = {iq: block_diag(vq[iq], fmask) for iq in groups}
    gb = {(i, q): _dg(lm[i, q], block_diag(seq[i]["bbar"][:, fs(q)], fmask), NT) for i, q in groups}
    gk = {(i, q): _dg(lm[i, q], block_diag(seq[i]["kbar"][:, fs(q)], fmask), NT) for i, q in groups}
    a_ab = {iq: jnp.where(strict_g, gb[iq][:t], 0.0) for iq in groups}
    a_ak = {iq: jnp.where(strict_g, gk[iq][:t], 0.0).astype(bf16) for iq in groups}
    a_rb = {iq: jnp.where(incl_g, gb[iq][t:], 0.0).astype(bf16) for iq in groups}
    a_rk = {iq: jnp.where(incl_g, gk[iq][t:], 0.0).astype(bf16) for iq in groups}
    minv = {iq: eye_g + jnp.where(first_level_g, a_ab[iq], 0.0) for iq in groups}
    for lvl in range(1, tmask_ref.shape[0]):
        tmp = {iq: _dg(minv[iq].astype(bf16), block_diag(a_ab[iq], tmask_ref[lvl])) for iq in groups}
        minv = {iq: minv[iq] + _dg(tmp[iq].astype(bf16), block_diag(minv[iq], tmask_ref[0])) for iq in groups}
    akv = {iq: _dg(a_ak[iq], vbd[iq]) for iq in groups}
    s_old = {(i, q): srs_ref[i, q] for i, q in groups}
    ls = {iq: _dg(lm[iq], s_old[iq].astype(bf16), NT) for iq in groups}
    z = {iq: _dg(minv[iq].astype(bf16), block_diag(ls[iq][:t] + akv[iq], fmask)) for iq in groups}
    yq = {iq: ls[iq][t:] + _dg(a_rb[iq], block_diag(z[iq], fmask)) + _dg(a_rk[iq], vbd[iq]) for iq in groups}
    for i, q in groups:
        zv = jnp.concatenate([z[i, q], vq[i, q]], axis=0).astype(bf16)
        rm = jnp.concatenate([seq[i]["bbar"][:, fs(q)], seq[i]["kbar"][:, fs(q)]], axis=0).astype(bf16)
        upd = jnp.where(state_diag, _dg(zv, rm, TN), 0.0)
        srs_ref[i, q] = (s_old[i, q] + upd) * seq[i]["g_last"][:, fs(q)]

    gseq = []
    for i in range(nb):
        q = p_ref[i, :, Q_OFF:K_OFF]
        kg = p_ref[i, :, K_OFF:V_OFF]
        gd = p_ref[i, :, GD_OFF:GATE_OFF]
        la = _log_sigmoid(_dot3(gd, gup_ref) + gb_ref[...]) * (1.0 / GLA_GATE_NORM)
        if valid is not None:
            la = jnp.where(valid, la, 0.0)
            kg = jnp.where(valid, kg, 0.0)
        b = _dot_exact_lhs(tril, la)
        b_last = b[t - 1:t, :]
        gseq.append(dict(qg=(q * jnp.exp(b) * (GLA_DK ** -0.5)).astype(bf16), kgi=(kg * jnp.exp(-b)).astype(bf16),
                         kd=(kg * jnp.exp(b_last - b)).astype(bf16), eb=jnp.exp(b_last)))
    gs = lambda h: slice(h * GLA_DK, (h + 1) * GLA_DK)
    gv = {(i, h): p_ref[i, :, V_OFF + h * GLA_DV:V_OFF + (h + 1) * GLA_DV].astype(bf16) for i, h in gpairs}
    att = {(i, h): jnp.where(incl, _dg(gseq[i]["qg"][:, gs(h)], gseq[i]["kgi"][:, gs(h)], NT), 0.0).astype(bf16)
           for i, h in gpairs}
    gs_old = {(i, h): sg_ref[i, h] for i, h in gpairs}
    go = {(i, h): _dg(att[i, h], gv[i, h]) + _dg(gseq[i]["qg"][:, gs(h)], gs_old[i, h].astype(bf16))
          for i, h in gpairs}
    for i, h in gpairs:
        e_col = jnp.sum(jnp.where(rk64 == ck64, gseq[i]["eb"][:, gs(h)], 0.0), axis=1, keepdims=True)
        sg_ref[i, h] = e_col * gs_old[i, h] + _dg(gseq[i]["kd"][:, gs(h)], gv[i, h], TN)

    inv_k = 1.0 / RWKV_HEAD
    for i in range(nb):
        y = jnp.concatenate([yq[i, q] for q in range(n_groups)], axis=1)
        mean = head_sum(y) * inv_k
        dlt = y - mean
        var = head_sum(dlt * dlt) * inv_k
        yn = dlt * lax.rsqrt(var + RWKV_GN_EPS) * lnw_ref[...] + lnb_ref[...]
        o_r = yn + head_sum(seq[i]["r"] * seq[i]["k2"] * rk_ref[...]) * seq[i]["v"]
        o_g = jnp.concatenate([_rmsnorm(go[i, h], gn_ref[...]) for h in range(GLA_HEADS)], axis=1)
        gate = p_ref[i, :, GATE_OFF:EVEN_PROJ_PAD]
        og_ref[i] = jnp.concatenate([o_r, o_g], axis=1) * (gate * _sigmoid(gate))

    @pl.when(c == pl.num_programs(1) - 1)
    def _():
        for i in range(nb):
            for q in range(n_groups):
                for j in range(HEAD_GROUP):
                    lo = j * RWKV_HEAD
                    sr_ref[i, q * HEAD_GROUP + j] = srs_ref[i, q, lo:lo + RWKV_HEAD, lo:lo + RWKV_HEAD]


def _group_masks(t):
    rows = np.arange(HEAD_GROUP * t)[:, None]
    fmask = rows // t == np.arange(HEAD_GROUP * RWKV_HEAD)[None, :] // RWKV_HEAD
    lanes = np.arange(HEAD_GROUP * t)[None, :]
    same_head = rows // t == lanes // t
    r, s = rows % t, lanes % t
    levels = [same_head]
    blk = 2
    while blk < t:
        levels.append(same_head & (r // (2 * blk) == s // (2 * blk)) & ((r // blk) % 2 == 1) & ((s // blk) % 2 == 0))
        blk *= 2
    return jnp.asarray(fmask, bf16), jnp.asarray(np.stack(levels), bf16)


def _even_mix(p, layer, n_layers, states, prev_outs, wts, t, n_valid, nb):
    bsz, lp, _ = p.shape
    nc = lp // t
    const = lambda shape: pl.BlockSpec(shape, lambda b, c: (0,) * len(shape))
    fmask, tmask = _group_masks(t)
    n_groups = RWKV_HEADS // HEAD_GROUP
    slab = HEAD_GROUP * RWKV_HEAD
    sr_block = (None, nb, RWKV_HEADS, RWKV_HEAD, RWKV_HEAD)
    sg_block = (None, nb, GLA_HEADS, GLA_DK, GLA_DV)
    layer_map = lambda b, c: (layer, b, 0, 0, 0)
    state_specs, state_args = [], []
    if states is not None:
        shift, sr0, sg0 = states
        state_specs = [pl.BlockSpec((None, nb, 1, RWKV_PROJ), lambda b, c: (layer, b, 0, 0)),
                       pl.BlockSpec(sr_block, layer_map), pl.BlockSpec(sg_block, layer_map)]
        state_args = [shift.reshape(shift.shape[0], bsz, 1, RWKV_PROJ), sr0, sg0]
    alias_specs, alias_args, aliases = [], [], {}
    if prev_outs is not None:
        alias_specs = [pl.BlockSpec(memory_space=pl.ANY)] * 2
        alias_args = list(prev_outs)
        first = 1 + len(state_args) + N_EVEN_WEIGHTS
        aliases = {first: 1, first + 1: 2}
    return pl.pallas_call(
        functools.partial(_even_mix_kernel, t=t, n_valid=n_valid, nb=nb, has_state=states is not None,
                          n_alias=len(alias_args)),
        grid=(bsz // nb, nc),
        in_specs=[
            pl.BlockSpec((nb, t, EVEN_PROJ_PAD), lambda b, c: (b, c, 0)),
            *state_specs,
            const((1, RWKV_PROJ)),
            const((1, RWKV_W)), const((2, 2 * RWKV_LORA, RWKV_W)),
            const((1, RWKV_W)), const((2, 2 * RWKV_LORA, RWKV_W)),
            const((1, RWKV_W)), const((1, RWKV_W)), const((1, RWKV_W)),
            const((1, RWKV_W)), const((1, RWKV_W)),
            const((2, GLA_LORA_PAD, GLA_KW)), const((1, GLA_KW)), const((1, GLA_DV)),
            const((LANE, LANE)), const(fmask.shape), const(tmask.shape),
            *alias_specs,
        ],
        out_specs=[
            pl.BlockSpec((nb, t, EVEN_WIDTH), lambda b, c: (b, c, 0)),
            pl.BlockSpec(sr_block, layer_map),
            pl.BlockSpec(sg_block, layer_map),
        ],
        out_shape=[
            jax.ShapeDtypeStruct((bsz, lp, EVEN_WIDTH), f32),
            jax.ShapeDtypeStruct((n_layers, bsz, RWKV_HEADS, RWKV_HEAD, RWKV_HEAD), f32),
            jax.ShapeDtypeStruct((n_layers, bsz, GLA_HEADS, GLA_DK, GLA_DV), f32),
        ],
        input_output_aliases=aliases,
        scratch_shapes=[pltpu.VMEM((nb, 1, RWKV_PROJ), f32), pltpu.VMEM((nb, n_groups, slab, slab), f32)],
        compiler_params=pltpu.CompilerParams(
            dimension_semantics=("arbitrary", "arbitrary"), vmem_limit_bytes=VMEM_LIMIT_BYTES),
        name="even_mix",
    )(p, *state_args, *wts, fmask, tmask, *alias_args)


def _odd_mix_kernel(*refs, tb, n_valid, nb, has_cache, n_alias, pos_major):
    p_ref = refs[0]
    n_in = 2 if has_cache else 1
    cw_ref, cb_ref, lnw_ref, lnb_ref = refs[n_in:n_in + 4]
    o_ref, cache_out_ref, bases_ref, ext_ref, y_ref = refs[n_in + 4 + n_alias:]
    seq_view = (lambda ref, i: ref.at[:, i]) if pos_major else (lambda ref, i: ref.at[i])
    for i in range(nb):
        _odd_mix_one(p_ref.at[i], seq_view(refs[1], i) if has_cache else None, cw_ref, cb_ref, lnw_ref, lnb_ref,
                     o_ref.at[i], seq_view(cache_out_ref, i), bases_ref.at[i], ext_ref, y_ref,
                     tb=tb, n_valid=n_valid)


def _odd_mix_one(p_ref, cache_ref, cw_ref, cb_ref, lnw_ref, lnb_ref, o_ref, cache_out_ref, base_ref, ext_ref,
                 y_ref, *, tb, n_valid):
    c = pl.program_id(1)
    pad = CONV_HIST_PAD - CONV_HIST
    rows = CONV_HIST_PAD + tb

    @pl.when(c == 0)
    def _():
        if cache_ref is not None:
            base_ref[0:SUBLANE, :] = jnp.zeros((SUBLANE, CONV_C), f32)
            base_ref[pad:CONV_HIST_PAD, :] = cache_ref[...]
        else:
            base_ref[0:CONV_HIST_PAD, :] = jnp.zeros((CONV_HIST_PAD, CONV_C), f32)
        base_ref[rows:rows + SUBLANE, :] = jnp.zeros((SUBLANE, CONV_C), f32)

    base_ref[CONV_HIST_PAD:rows, :] = p_ref[:, 0:CONV_C] * _sigmoid(p_ref[:, CONV_C:2 * CONV_C])
    for sh in range(SUBLANE):
        for ct in range(CONV_C // LANE):
            ext_ref[sh, ct] = base_ref[sh:sh + rows, ct * LANE:(ct + 1) * LANE]

    taps = [[j for j in range(CONV_W) if (pad + j) % SUBLANE == sh] for sh in range(SUBLANE)]
    first = [((pad + taps[sh][0]) - sh) // SUBLANE for sh in range(SUBLANE)]
    depth = [((pad + taps[sh][-1]) - sh) // SUBLANE - first[sh] + 1 for sh in range(SUBLANE)]
    for ct in range(CONV_C // LANE):
        ls = slice(ct * LANE, (ct + 1) * LANE)
        bias = jnp.broadcast_to(cb_ref[:, ls], (SUBLANE, LANE))

        def tile(sh, idx, ct=ct):
            return ext_ref[sh, ct, pl.ds(pl.multiple_of(idx * SUBLANE, SUBLANE), SUBLANE), :]

        def body(k, carry, ls=ls, bias=bias, tile=tile):
            new_carry, partial = [], []
            for sh in range(SUBLANE):
                tiles = list(carry[sh]) + [tile(sh, k + first[sh] + depth[sh] - 1)]
                acc = None
                for j in taps[sh]:
                    term = cw_ref[j:j + 1, ls] * tiles[((pad + j) - sh) // SUBLANE - first[sh]]
                    acc = term if acc is None else acc + term
                partial.append(acc)
                new_carry.append(tuple(tiles[1:]))
            while len(partial) > 1:
                partial = [partial[i] + partial[i + 1] for i in range(0, len(partial), 2)]
            y_ref[pl.ds(pl.multiple_of(k * SUBLANE, SUBLANE), SUBLANE), ls] = partial[0] + bias
            return tuple(new_carry)

        init = tuple(tuple(tile(sh, first[sh] + a) for a in range(depth[sh] - 1)) for sh in range(SUBLANE))
        lax.fori_loop(0, tb // SUBLANE, body, init, unroll=min(CONV_UNROLL, tb // SUBLANE))

    y = y_ref[...]
    mu = jnp.mean(y, axis=-1, keepdims=True)
    d = y - mu
    var = jnp.mean(d * d, axis=-1, keepdims=True)
    yn = d * lax.rsqrt(var + LN_EPS) * lnw_ref[...] + lnb_ref[...]
    g = p_ref[:, 2 * CONV_C:3 * CONV_C]
    o_ref[...] = (yn * _sigmoid(yn)) * (g * _sigmoid(g))

    @pl.when(c == pl.num_programs(1) - 1)
    def _():
        cache_out_ref[...] = base_ref[pad + n_valid:pad + n_valid + CONV_HIST, :]

    base_ref[0:CONV_HIST_PAD, :] = base_ref[tb:rows, :]


def _odd_mix(p, layer, n_layers, cache, prev_out, cw, cb, lnw, lnb, tb, n_valid, nb, pos_major):
    bsz, lp, _ = p.shape
    nc = lp // tb
    const = lambda shape: pl.BlockSpec(shape, lambda b, c: (0,) * len(shape))
    if pos_major:
        cache_block, cache_shape = (None, CONV_HIST, nb, CONV_C), (n_layers, CONV_HIST, bsz, CONV_C)
        layer_map = lambda b, c: (layer, 0, b, 0)
    else:
        cache_block, cache_shape = (None, nb, CONV_HIST, CONV_C), (n_layers, bsz, CONV_HIST, CONV_C)
        layer_map = lambda b, c: (layer, b, 0, 0)
    cache_specs = [] if cache is None else [pl.BlockSpec(cache_block, layer_map)]
    cache_args = [] if cache is None else [cache]
    alias_specs = [] if prev_out is None else [pl.BlockSpec(memory_space=pl.ANY)]
    alias_args = [] if prev_out is None else [prev_out]
    aliases = {} if prev_out is None else {1 + len(cache_args) + 4: 1}
    return pl.pallas_call(
        functools.partial(_odd_mix_kernel, tb=tb, n_valid=n_valid, nb=nb, has_cache=cache is not None,
                          n_alias=len(alias_args), pos_major=pos_major),
        grid=(bsz // nb, nc),
        in_specs=[
            pl.BlockSpec((nb, tb, ODD_PROJ), lambda b, c: (b, c, 0)),
            *cache_specs,
            const((CONV_W, CONV_C)), const((1, CONV_C)), const((1, CONV_C)), const((1, CONV_C)),
            *alias_specs,
        ],
        out_specs=[
            pl.BlockSpec((nb, tb, CONV_C), lambda b, c: (b, c, 0)),
            pl.BlockSpec(cache_block, layer_map),
        ],
        out_shape=[
            jax.ShapeDtypeStruct((bsz, lp, CONV_C), f32),
            jax.ShapeDtypeStruct(cache_shape, f32),
        ],
        input_output_aliases=aliases,
        scratch_shapes=[pltpu.VMEM((nb, CONV_HIST_PAD + tb + SUBLANE, CONV_C), f32),
                        pltpu.VMEM((SUBLANE, CONV_C // LANE, CONV_HIST_PAD + tb, LANE), f32),
                        pltpu.VMEM((tb, CONV_C), f32)],
        compiler_params=pltpu.CompilerParams(
            dimension_semantics=("arbitrary", "arbitrary"), vmem_limit_bytes=VMEM_LIMIT_BYTES),
        name="odd_mix",
    )(p, *cache_args, cw, cb, lnw, lnb, *alias_args)


def _row(x):
    return x.reshape(1, -1)


def kernel(x_prompt, x_sample, state_rwkv_shift, state_rwkv, state_gla, cache_conv, norm_even, w_in_even, rwkv_mu, rwkv_w0, rwkv_w_up, rwkv_a0, rwkv_a_up, rwkv_k_k, rwkv_k_a, rwkv_r_k, rwkv_ln_w, rwkv_ln_b, gla_g_up, gla_g_b, gla_norm, w_out_even, norm_odd, w_in_odd, b_in_odd, conv_w, conv_b, conv_ln_w, conv_ln_b, w_out_odd, final_norm):
    bp, lp, _ = x_prompt.shape
    bs, ls, _ = x_sample.shape
    depth = norm_even.shape[0] + norm_odd.shape[0]
    xs_pad = jnp.pad(x_sample, ((0, 0), (0, SAMPLE_PAD_LEN - ls), (0, 0)))
    groups = [
        dict(x=x_prompt.reshape(bp * lp, D_MODEL), b=bp, l=lp, nv=lp, t=PROMPT_CHUNK, tb=CONV_BLOCK, prompt=True,
             nb=PROMPT_SEQS_PER_STEP, cnb=1),
        dict(x=xs_pad.reshape(bs * SAMPLE_PAD_LEN, D_MODEL), b=bs, l=SAMPLE_PAD_LEN, nv=ls, t=SAMPLE_PAD_LEN,
             tb=SAMPLE_PAD_LEN, prompt=False, nb=SAMPLE_SEQS_PER_STEP, cnb=SAMPLE_SEQS_PER_STEP),
    ]
    bd = (jnp.arange(LANE)[:, None] // RWKV_HEAD == jnp.arange(LANE)[None, :] // RWKV_HEAD).astype(bf16)
    zeros_lora = jnp.zeros((RWKV_LORA, RWKV_W), f32)
    outs = [dict(shift=[], states=None, conv=None) for _ in groups]
    n_even, n_odd = norm_even.shape[0], norm_odd.shape[0]

    proj = []
    for layer in range(depth):
        i = layer // 2
        if layer % 2 == 0:
            w = w_in_even[i]
            w_pad = jnp.concatenate(
                [w[:, :RWKV_PROJ + GLA_PROJ], jnp.zeros((D_MODEL, GLA_LORA_PAD - GLA_LORA), f32),
                 w[:, RWKV_PROJ + GLA_PROJ:]], axis=1).astype(bf16)
            proj.append((_row(norm_even[i]), w_pad, jnp.zeros((1, EVEN_PROJ_PAD), f32), w_out_even[i].astype(bf16)))
        else:
            proj.append((_row(norm_odd[i]), w_in_odd[i].astype(bf16), _row(b_in_odd[i]), w_out_odd[i].astype(bf16)))
    for g in groups:
        g["p"] = _inproj(g["x"], *proj[0][:3])

    def next_layer(g, layer, o2d):
        if layer == depth - 1:
            g["x"] = _outproj(o2d, g["x"], proj[layer][3], _row(final_norm))
        else:
            g["x"], g["p"] = _midproj(o2d, g["x"], proj[layer][3], *proj[layer + 1][:3])

    for layer in range(depth):
        i = layer // 2
        if layer % 2 == 0:
            wts = (
                _row(rwkv_mu[i]), _row(rwkv_w0[i]),
                _split_weight(jnp.concatenate([rwkv_w_up[i], zeros_lora], axis=0)),
                _row(rwkv_a0[i]), _split_weight(jnp.concatenate([zeros_lora, rwkv_a_up[i]], axis=0)),
                _row(rwkv_k_k[i]), _row(rwkv_k_a[i]), _row(rwkv_r_k[i]), _row(rwkv_ln_w[i]), _row(rwkv_ln_b[i]),
                _split_weight(jnp.concatenate([gla_g_up[i], jnp.zeros((GLA_LORA_PAD - GLA_LORA, GLA_KW), f32)],
                                              axis=0)),
                _row(gla_g_b[i]), _row(gla_norm[i]), bd,
            )
            for gi, g in enumerate(groups):
                p = g["p"].reshape(g["b"], g["l"], EVEN_PROJ_PAD)
                states = None if g["prompt"] else (state_rwkv_shift, state_rwkv, state_gla)
                og, sr, sg = _even_mix(p, i, n_even, states, outs[gi]["states"], wts, g["t"],
                                       min(g["nv"], g["t"]), g["nb"])
                outs[gi]["shift"].append(p[:, g["nv"] - 1, :RWKV_PROJ])
                outs[gi]["states"] = (sr, sg)
                next_layer(g, layer, og.reshape(-1, EVEN_WIDTH))
        else:
            for gi, g in enumerate(groups):
                p = g["p"].reshape(g["b"], g["l"], ODD_PROJ)
                cache = None if g["prompt"] else jnp.transpose(cache_conv, (0, 2, 1, 3))
                y, outs[gi]["conv"] = _odd_mix(p, i, n_odd, cache, outs[gi]["conv"], conv_w[i], _row(conv_b[i]),
                                               _row(conv_ln_w[i]), _row(conv_ln_b[i]), g["tb"],
                                               min(g["nv"], g["tb"]), g["cnb"], not g["prompt"])
                next_layer(g, layer, y.reshape(-1, CONV_C))

    y_prompt = groups[0]["x"].reshape(bp, lp, D_MODEL)
    y_sample = groups[1]["x"].reshape(bs, SAMPLE_PAD_LEN, D_MODEL)[:, :ls]
    op, os_ = outs
    return (y_prompt, y_sample,
            jnp.stack(op["shift"]), jnp.stack(os_["shift"]),
            op["states"][0], os_["states"][0],
            op["states"][1], os_["states"][1],
            op["conv"], jnp.transpose(os_["conv"], (0, 2, 1, 3)))
```

```python
import functools
import math

import jax
import jax.numpy as jnp
import numpy as np
from jax import lax
from jax.experimental import pallas as pl
from jax.experimental.pallas import tpu as pltpu

f32 = jnp.float32
bf16 = jnp.bfloat16

D_MODEL = 1024
RWKV_HEAD = 64
RWKV_HEADS = 8
RWKV_W = 512
RWKV_LORA = 64
RWKV_PROJ = 3 * RWKV_W + 2 * RWKV_LORA
GLA_HEADS = 4
GLA_DK = 64
GLA_DV = 128
GLA_KW = GLA_HEADS * GLA_DK
GLA_VW = GLA_HEADS * GLA_DV
GLA_LORA = 16
GLA_LORA_PAD = 128
GLA_PROJ = 2 * GLA_KW + GLA_VW + GLA_LORA
EVEN_WIDTH = RWKV_W + GLA_VW
Q_OFF = RWKV_PROJ
K_OFF = Q_OFF + GLA_KW
V_OFF = K_OFF + GLA_KW
GD_OFF = V_OFF + GLA_VW
GATE_OFF = GD_OFF + GLA_LORA_PAD
EVEN_PROJ_PAD = GATE_OFF + EVEN_WIDTH
CONV_C = 1024
CONV_W = 31
CONV_HIST = CONV_W - 1
CONV_HIST_PAD = 32
ODD_PROJ = 3 * CONV_C

RMS_EPS = 1e-6
LN_EPS = 1e-5
RWKV_GN_EPS = 64e-5
GLA_GATE_NORM = 16.0
PROMPT_CHUNK = 64
SAMPLE_PAD_LEN = 8
PROMPT_SEQS_PER_STEP = 8
SAMPLE_SEQS_PER_STEP = 16
CONV_BLOCK = 256
CONV_UNROLL = 2
LANE = 128
SUBLANE = 8
BF16_SUBLANES = 16
HEAD_GROUP = 2
VMEM_LIMIT_BYTES = 48 * 1024 * 1024

NN = ((1,), (0,))
NT = ((1,), (1,))
TN = ((0,), (0,))


def _dg(a, b, dims=NN):
    return lax.dot_general(a, b, (dims, ((), ())), preferred_element_type=f32)


def _dot1(a, b, dims=NN):
    return _dg(a.astype(bf16), b.astype(bf16), dims)


def _split2(x):
    hi = x.astype(bf16)
    lo = (x - hi.astype(f32)).astype(bf16)
    return hi, lo


def _split3(x):
    hi = x.astype(bf16)
    r1 = x - hi.astype(f32)
    mid = r1.astype(bf16)
    lo = (r1 - mid.astype(f32)).astype(bf16)
    return hi, mid, lo


def _dot3(a, b_ref):
    ah, al = _split2(a)
    bh, bl = b_ref[0], b_ref[1]
    return _dg(ah, bh) + _dg(ah, bl) + _dg(al, bh)


def _split_weight(w):
    hi = w.astype(bf16)
    return jnp.stack([hi, (w - hi.astype(f32)).astype(bf16)])


def _dot_exact_lhs(m, x):
    hi, mid, lo = _split3(x)
    return _dg(m, hi) + _dg(m, mid) + _dg(m, lo)


def _sigmoid(x):
    return 0.5 * jnp.tanh(0.5 * x) + 0.5


def _log_sigmoid(x):
    return jnp.minimum(x, 0.0) - jnp.log(1.0 + jnp.exp(-jnp.abs(x)))


def _rmsnorm(x, g):
    return x * lax.rsqrt(jnp.mean(x * x, axis=-1, keepdims=True) + RMS_EPS) * g


def _inproj_kernel(x_ref, g_ref, w_ref, b_ref, o_ref):
    h = _rmsnorm(x_ref[...], g_ref[...])
    o_ref[...] = _dg(h.astype(bf16), w_ref[...]) + b_ref[...]


def _inproj(x2d, g, w, b):
    m, n = x2d.shape[0], w.shape[1]
    tm = 512
    return pl.pallas_call(
        _inproj_kernel,
        grid=(m // tm,),
        in_specs=[
            pl.BlockSpec((tm, D_MODEL), lambda i: (i, 0)),
            pl.BlockSpec((1, D_MODEL), lambda i: (0, 0)),
            pl.BlockSpec((D_MODEL, n), lambda i: (0, 0)),
            pl.BlockSpec((1, n), lambda i: (0, 0)),
        ],
        out_specs=pl.BlockSpec((tm, n), lambda i: (i, 0)),
        out_shape=jax.ShapeDtypeStruct((m, n), f32),
        compiler_params=pltpu.CompilerParams(
            dimension_semantics=("arbitrary",), vmem_limit_bytes=VMEM_LIMIT_BYTES),
        name="inproj",
    )(x2d, g, w, b)


def _outproj_kernel(o_ref, x_ref, w_ref, fn_ref, y_ref):
    y_ref[...] = _rmsnorm(x_ref[...] + _dg(o_ref[...].astype(bf16), w_ref[...]), fn_ref[...])


def _outproj(o2d, x2d, w, fn):
    m = x2d.shape[0]
    tm = 512
    return pl.pallas_call(
        _outproj_kernel,
        grid=(m // tm,),
        in_specs=[
            pl.BlockSpec((tm, D_MODEL), lambda i: (i, 0)),
            pl.BlockSpec((tm, D_MODEL), lambda i: (i, 0)),
            pl.BlockSpec((D_MODEL, D_MODEL), lambda i: (0, 0)),
            pl.BlockSpec((1, D_MODEL), lambda i: (0, 0)),
        ],
        out_specs=pl.BlockSpec((tm, D_MODEL), lambda i: (i, 0)),
        out_shape=jax.ShapeDtypeStruct((m, D_MODEL), f32),
        compiler_params=pltpu.CompilerParams(
            dimension_semantics=("arbitrary",), vmem_limit_bytes=VMEM_LIMIT_BYTES),
        name="outproj",
    )(o2d, x2d, w, fn)


def _midproj_kernel(o_ref, x_ref, wo_ref, g_ref, wi_ref, b_ref, xn_ref, p_ref):
    y = x_ref[...] + _dg(o_ref[...].astype(bf16), wo_ref[...])
    xn_ref[...] = y
    p_ref[...] = _dg(_rmsnorm(y, g_ref[...]).astype(bf16), wi_ref[...]) + b_ref[...]


def _midproj(o2d, x2d, w_out, g, w_in, b):
    m, n = x2d.shape[0], w_in.shape[1]
    tm = 512
    row_block = pl.BlockSpec((tm, D_MODEL), lambda i: (i, 0))
    resident = lambda shape: pl.BlockSpec(shape, lambda i: (0, 0), pipeline_mode=pl.Buffered(1))
    return pl.pallas_call(
        _midproj_kernel,
        grid=(m // tm,),
        in_specs=[
            row_block, row_block,
            resident((D_MODEL, D_MODEL)), resident((1, D_MODEL)), resident((D_MODEL, n)), resident((1, n)),
        ],
        out_specs=[row_block, pl.BlockSpec((tm, n), lambda i: (i, 0))],
        out_shape=[jax.ShapeDtypeStruct((m, D_MODEL), f32), jax.ShapeDtypeStruct((m, n), f32)],
        compiler_params=pltpu.CompilerParams(
            dimension_semantics=("arbitrary",), vmem_limit_bytes=VMEM_LIMIT_BYTES),
        name="midproj",
    )(o2d, x2d, w_out, g, w_in, b)


N_EVEN_WEIGHTS = 16


def _even_mix_kernel(*refs, t, n_valid, nb, has_state, n_alias):
    p_ref = refs[0]
    n_in = 1 + (3 if has_state else 0)
    (mu_ref, w0_ref, wup_ref, a0_ref, aup_ref, kk_ref, ka_ref, rk_ref, lnw_ref, lnb_ref, gup_ref, gb_ref, gn_ref,
     bd_ref, fmask_ref, tmask_ref) = refs[n_in:n_in + N_EVEN_WEIGHTS]
    og_ref, sr_ref, sg_ref, prev_ref, srs_ref = refs[n_in + N_EVEN_WEIGHTS + n_alias:]
    c = pl.program_id(1)
    n_groups = RWKV_HEADS // HEAD_GROUP

    @pl.when(c == 0)
    def _():
        if not has_state:
            sg_ref[...] = jnp.zeros(sg_ref.shape, f32)
            prev_ref[...] = jnp.zeros(prev_ref.shape, f32)
            srs_ref[...] = jnp.zeros(srs_ref.shape, f32)
            return
        shift_ref, sr_in_ref, sg_in_ref = refs[1:4]
        sg_ref[...] = sg_in_ref[...]
        prev_ref[...] = shift_ref[...]
        for i in range(nb):
            for q in range(n_groups):
                rows_ = []
                for j in range(HEAD_GROUP):
                    parts = [jnp.zeros((RWKV_HEAD, RWKV_HEAD), f32)] * HEAD_GROUP
                    parts[j] = sr_in_ref[i, q * HEAD_GROUP + j]
                    rows_.append(jnp.concatenate(parts, axis=1))
                srs_ref[i, q] = jnp.concatenate(rows_, axis=0)

    row = lax.broadcasted_iota(jnp.int32, (t, t), 0)
    col = lax.broadcasted_iota(jnp.int32, (t, t), 1)
    incl = row >= col
    tril = jnp.where(incl, 1.0, 0.0).astype(bf16)
    rowv = lax.broadcasted_iota(jnp.int32, (t, 1), 0)
    valid = None if n_valid == t else rowv < n_valid
    bd = bd_ref[...]
    rk64 = lax.broadcasted_iota(jnp.int32, (GLA_DK, GLA_DK), 0)
    ck64 = lax.broadcasted_iota(jnp.int32, (GLA_DK, GLA_DK), 1)

    def head_sum(x):
        tiles = x.shape[1] // LANE
        stacked = jnp.concatenate([x[:, m * LANE:(m + 1) * LANE] for m in range(tiles)], axis=0)
        hi, lo = _split2(stacked)
        s = _dg(hi, bd) + _dg(lo, bd)
        return jnp.concatenate([s[m * t:(m + 1) * t] for m in range(tiles)], axis=1)

    gpairs = [(i, h) for i in range(nb) for h in range(GLA_HEADS)]

    seq = []
    for i in range(nb):
        pr = p_ref[i, :, 0:RWKV_PROJ]
        p_prev = jnp.where(rowv == 0, prev_ref[i], pltpu.roll(pr, 1, 0))
        prev_ref[i] = pr[t - 1:t, :]
        xm = pr + (p_prev - pr) * mu_ref[...]
        r = xm[:, 0:RWKV_W]
        k = xm[:, RWKV_W:2 * RWKV_W]
        v = xm[:, 2 * RWKV_W:3 * RWKV_W]
        xwa = xm[:, 3 * RWKV_W:RWKV_PROJ]
        wl = w0_ref[...] + _dot3(jnp.tanh(xwa), wup_ref)
        al = a0_ref[...] + _dot3(xwa, aup_ref)
        ld = (-math.exp(-0.5)) * _sigmoid(wl)
        a = _sigmoid(al)
        kkr = k * kk_ref[...]
        kkn = kkr * lax.rsqrt(jnp.maximum(head_sum(kkr * kkr), 1e-24))
        k2 = k * (1.0 + (a - 1.0) * ka_ref[...])
        if valid is not None:
            ld = jnp.where(valid, ld, 0.0)
            kkn = jnp.where(valid, kkn, 0.0)
            k2 = jnp.where(valid, k2, 0.0)
        c_in = _dot_exact_lhs(tril, ld)
        g_in = jnp.exp(c_in)
        ginv = jnp.exp(-c_in)
        seq.append(dict(r=r, k2=k2, v=v, g_last=g_in[t - 1:t, :],
                        abar=-kkn * jnp.exp(c_in - ld), bbar=kkn * a * ginv, kbar=k2 * ginv, rbar=r * g_in))

    fw = HEAD_GROUP * RWKV_HEAD
    tw = HEAD_GROUP * t
    rt = lax.broadcasted_iota(jnp.int32, (t, tw), 0)
    st = lax.broadcasted_iota(jnp.int32, (t, tw), 1) % t
    strict_g = rt > st
    incl_g = rt >= st
    eye_g = jnp.where(rt == st, 1.0, 0.0).astype(f32)
    first_level_g = (rt // 2 == st // 2) & (rt % 2 == 1) & (st % 2 == 0)
    rs = lax.broadcasted_iota(jnp.int32, (fw, fw), 0) // RWKV_HEAD
    cs = lax.broadcasted_iota(jnp.int32, (fw, fw), 1) // RWKV_HEAD
    state_diag = rs == cs

    def block_diag(x, mask):
        if x.shape[0] % BF16_SUBLANES == 0:
            xb = x.astype(bf16)
            return jnp.concatenate([xb] * HEAD_GROUP, axis=0) * mask
        return (jnp.concatenate([x] * HEAD_GROUP, axis=0) * mask.astype(f32)).astype(bf16)

    fmask = fmask_ref[...]
    groups = [(i, q) for i in range(nb) for q in range(RWKV_HEADS // HEAD_GROUP)]
    fs = lambda q: slice(q * fw, (q + 1) * fw)
    lm = {(i, q): jnp.concatenate([seq[i]["abar"][:, fs(q)], seq[i]["rbar"][:, fs(q)]], axis=0).astype(bf16)
          for i, q in groups}
    vq = {(i, q): seq[i]["v"][:, fs(q)] for i, q in groups}
    vbd = {iq: block_diag(vq[iq], fmask) for iq in groups}
    gb = {(i, q): _dg(lm[i, q], block_diag(seq[i]["bbar"][:, fs(q)], fmask), NT) for i, q in groups}
    gk = {(i, q): _dg(lm[i, q], block_diag(seq[i]["kbar"][:, fs(q)], fmask), NT) for i, q in groups}
    a_ab = {iq: jnp.where(strict_g, gb[iq][:t], 0.0) for iq in groups}
    a_ak = {iq: jnp.where(strict_g, gk[iq][:t], 0.0).astype(bf16) for iq in groups}
    a_rb = {iq: jnp.where(incl_g, gb[iq][t:], 0.0).astype(bf16) for iq in groups}
    a_rk = {iq: jnp.where(incl_g, gk[iq][t:], 0.0).astype(bf16) for iq in groups}
    minv = {iq: eye_g + jnp.where(first_level_g, a_ab[iq], 0.0) for iq in groups}
    for lvl in range(1, tmask_ref.shape[0]):
        tmp = {iq: _dg(minv[iq].astype(bf16), block_diag(a_ab[iq], tmask_ref[lvl])) for iq in groups}
        minv = {iq: minv[iq] + _dg(tmp[iq].astype(bf16), block_diag(minv[iq], tmask_ref[0])) for iq in groups}
    akv = {iq: _dg(a_ak[iq], vbd[iq]) for iq in groups}
    s_old = {(i, q): srs_ref[i, q] for i, q in groups}
    ls = {iq: _dg(lm[iq], s_old[iq].astype(bf16), NT) for iq in groups}
    z = {iq: _dg(minv[iq].astype(bf16), block_diag(ls[iq][:t] + akv[iq], fmask)) for iq in groups}
    yq = {iq: ls[iq][t:] + _dg(a_rb[iq], block_diag(z[iq], fmask)) + _dg(a_rk[iq], vbd[iq]) for iq in groups}
    for i, q in groups:
        zv = jnp.concatenate([z[i, q], vq[i, q]], axis=0).astype(bf16)
        rm = jnp.concatenate([seq[i]["bbar"][:, fs(q)], seq[i]["kbar"][:, fs(q)]], axis=0).astype(bf16)
        upd = jnp.where(state_diag, _dg(zv, rm, TN), 0.0)
        srs_ref[i, q] = (s_old[i, q] + upd) * seq[i]["g_last"][:, fs(q)]

    gseq = []
    for i in range(nb):
        q = p_ref[i, :, Q_OFF:K_OFF]
        kg = p_ref[i, :, K_OFF:V_OFF]
        gd = p_ref[i, :, GD_OFF:GATE_OFF]
        la = _log_sigmoid(_dot3(gd, gup_ref) + gb_ref[...]) * (1.0 / GLA_GATE_NORM)
        if valid is not None:
            la = jnp.where(valid, la, 0.0)
            kg = jnp.where(valid, kg, 0.0)
        b = _dot_exact_lhs(tril, la)
        b_last = b[t - 1:t, :]
        gseq.append(dict(qg=(q * jnp.exp(b) * (GLA_DK ** -0.5)).astype(bf16), kgi=(kg * jnp.exp(-b)).astype(bf16),
                         kd=(kg * jnp.exp(b_last - b)).astype(bf16), eb=jnp.exp(b_last)))
    gs = lambda h: slice(h * GLA_DK, (h + 1) * GLA_DK)
    gv = {(i, h): p_ref[i, :, V_OFF + h * GLA_DV:V_OFF + (h + 1) * GLA_DV].astype(bf16) for i, h in gpairs}
    att = {(i, h): jnp.where(incl, _dg(gseq[i]["qg"][:, gs(h)], gseq[i]["kgi"][:, gs(h)], NT), 0.0).astype(bf16)
           for i, h in gpairs}
    gs_old = {(i, h): sg_ref[i, h] for i, h in gpairs}
    go = {(i, h): _dg(att[i, h], gv[i, h]) + _dg(gseq[i]["qg"][:, gs(h)], gs_old[i, h].astype(bf16))
          for i, h in gpairs}
    for i, h in gpairs:
        e_col = jnp.sum(jnp.where(rk64 == ck64, gseq[i]["eb"][:, gs(h)], 0.0), axis=1, keepdims=True)
        sg_ref[i, h] = e_col * gs_old[i, h] + _dg(gseq[i]["kd"][:, gs(h)], gv[i, h], TN)

    inv_k = 1.0 / RWKV_HEAD
    for i in range(nb):
        y = jnp.concatenate([yq[i, q] for q in range(n_groups)], axis=1)
        mean = head_sum(y) * inv_k
        dlt = y - mean
        var = head_sum(dlt * dlt) * inv_k
        yn = dlt * lax.rsqrt(var + RWKV_GN_EPS) * lnw_ref[...] + lnb_ref[...]
        o_r = yn + head_sum(seq[i]["r"] * seq[i]["k2"] * rk_ref[...]) * seq[i]["v"]
        o_g = jnp.concatenate([_rmsnorm(go[i, h], gn_ref[...]) for h in range(GLA_HEADS)], axis=1)
        gate = p_ref[i, :, GATE_OFF:EVEN_PROJ_PAD]
        og_ref[i] = jnp.concatenate([o_r, o_g], axis=1) * (gate * _sigmoid(gate))

    @pl.when(c == pl.num_programs(1) - 1)
    def _():
        for i in range(nb):
            for q in range(n_groups):
                for j in range(HEAD_GROUP):
                    lo = j * RWKV_HEAD
                    sr_ref[i, q * HEAD_GROUP + j] = srs_ref[i, q, lo:lo + RWKV_HEAD, lo:lo + RWKV_HEAD]


def _group_masks(t):
    rows = np.arange(HEAD_GROUP * t)[:, None]
    fmask = rows // t == np.arange(HEAD_GROUP * RWKV_HEAD)[None, :] // RWKV_HEAD
    lanes = np.arange(HEAD_GROUP * t)[None, :]
    same_head = rows // t == lanes // t
    r, s = rows % t, lanes % t
    levels = [same_head]
    blk = 2
    while blk < t:
        levels.append(same_head & (r // (2 * blk) == s // (2 * blk)) & ((r // blk) % 2 == 1) & ((s // blk) % 2 == 0))
        blk *= 2
    return jnp.asarray(fmask, bf16), jnp.asarray(np.stack(levels), bf16)


def _even_mix(p, layer, n_layers, states, prev_outs, wts, t, n_valid, nb):
    bsz, lp, _ = p.shape
    nc = lp // t
    const = lambda shape: pl.BlockSpec(shape, lambda b, c: (0,) * len(shape))
    fmask, tmask = _group_masks(t)
    n_groups = RWKV_HEADS // HEAD_GROUP
    slab = HEAD_GROUP * RWKV_HEAD
    sr_block = (None, nb, RWKV_HEADS, RWKV_HEAD, RWKV_HEAD)
    sg_block = (None, nb, GLA_HEADS, GLA_DK, GLA_DV)
    layer_map = lambda b, c: (layer, b, 0, 0, 0)
    state_specs, state_args = [], []
    if states is not None:
        shift, sr0, sg0 = states
        state_specs = [pl.BlockSpec((None, nb, 1, RWKV_PROJ), lambda b, c: (layer, b, 0, 0)),
                       pl.BlockSpec(sr_block, layer_map), pl.BlockSpec(sg_block, layer_map)]
        state_args = [shift.reshape(shift.shape[0], bsz, 1, RWKV_PROJ), sr0, sg0]
    alias_specs, alias_args, aliases = [], [], {}
    if prev_outs is not None:
        alias_specs = [pl.BlockSpec(memory_space=pl.ANY)] * 2
        alias_args = list(prev_outs)
        first = 1 + len(state_args) + N_EVEN_WEIGHTS
        aliases = {first: 1, first + 1: 2}
    return pl.pallas_call(
        functools.partial(_even_mix_kernel, t=t, n_valid=n_valid, nb=nb, has_state=states is not None,
                          n_alias=len(alias_args)),
        grid=(bsz // nb, nc),
        in_specs=[
            pl.BlockSpec((nb, t, EVEN_PROJ_PAD), lambda b, c: (b, c, 0)),
            *state_specs,
            const((1, RWKV_PROJ)),
            const((1, RWKV_W)), const((2, 2 * RWKV_LORA, RWKV_W)),
            const((1, RWKV_W)), const((2, 2 * RWKV_LORA, RWKV_W)),
            const((1, RWKV_W)), const((1, RWKV_W)), const((1, RWKV_W)),
            const((1, RWKV_W)), const((1, RWKV_W)),
            const((2, GLA_LORA_PAD, GLA_KW)), const((1, GLA_KW)), const((1, GLA_DV)),
            const((LANE, LANE)), const(fmask.shape), const(tmask.shape),
            *alias_specs,
        ],
        out_specs=[
            pl.BlockSpec((nb, t, EVEN_WIDTH), lambda b, c: (b, c, 0)),
            pl.BlockSpec(sr_block, layer_map),
            pl.BlockSpec(sg_block, layer_map),
        ],
        out_shape=[
            jax.ShapeDtypeStruct((bsz, lp, EVEN_WIDTH), f32),
            jax.ShapeDtypeStruct((n_layers, bsz, RWKV_HEADS, RWKV_HEAD, RWKV_HEAD), f32),
            jax.ShapeDtypeStruct((n_layers, bsz, GLA_HEADS, GLA_DK, GLA_DV), f32),
        ],
        input_output_aliases=aliases,
        scratch_shapes=[pltpu.VMEM((nb, 1, RWKV_PROJ), f32), pltpu.VMEM((nb, n_groups, slab, slab), f32)],
        compiler_params=pltpu.CompilerParams(
            dimension_semantics=("arbitrary", "arbitrary"), vmem_limit_bytes=VMEM_LIMIT_BYTES),
        name="even_mix",
    )(p, *state_args, *wts, fmask, tmask, *alias_args)


def _odd_mix_kernel(*refs, tb, n_valid, nb, has_cache, n_alias, pos_major):
    p_ref = refs[0]
    n_in = 2 if has_cache else 1
    cw_ref, cb_ref, lnw_ref, lnb_ref = refs[n_in:n_in + 4]
    o_ref, cache_out_ref, bases_ref, ext_ref, y_ref = refs[n_in + 4 + n_alias:]
    seq_view = (lambda ref, i: ref.at[:, i]) if pos_major else (lambda ref, i: ref.at[i])
    for i in range(nb):
        _odd_mix_one(p_ref.at[i], seq_view(refs[1], i) if has_cache else None, cw_ref, cb_ref, lnw_ref, lnb_ref,
                     o_ref.at[i], seq_view(cache_out_ref, i), bases_ref.at[i], ext_ref, y_ref,
                     tb=tb, n_valid=n_valid)


def _odd_mix_one(p_ref, cache_ref, cw_ref, cb_ref, lnw_ref, lnb_ref, o_ref, cache_out_ref, base_ref, ext_ref,
                 y_ref, *, tb, n_valid):
    c = pl.program_id(1)
    pad = CONV_HIST_PAD - CONV_HIST
    rows = CONV_HIST_PAD + tb

    @pl.when(c == 0)
    def _():
        if cache_ref is not None:
            base_ref[0:SUBLANE, :] = jnp.zeros((SUBLANE, CONV_C), f32)
            base_ref[pad:CONV_HIST_PAD, :] = cache_ref[...]
        else:
            base_ref[0:CONV_HIST_PAD, :] = jnp.zeros((CONV_HIST_PAD, CONV_C), f32)
        base_ref[rows:rows + SUBLANE, :] = jnp.zeros((SUBLANE, CONV_C), f32)

    base_ref[CONV_HIST_PAD:rows, :] = p_ref[:, 0:CONV_C] * _sigmoid(p_ref[:, CONV_C:2 * CONV_C])
    for sh in range(SUBLANE):
        for ct in range(CONV_C // LANE):
            ext_ref[sh, ct] = base_ref[sh:sh + rows, ct * LANE:(ct + 1) * LANE]

    taps = [[j for j in range(CONV_W) if (pad + j) % SUBLANE == sh] for sh in range(SUBLANE)]
    first = [((pad + taps[sh][0]) - sh) // SUBLANE for sh in range(SUBLANE)]
    depth = [((pad + taps[sh][-1]) - sh) // SUBLANE - first[sh] + 1 for sh in range(SUBLANE)]
    for ct in range(CONV_C // LANE):
        ls = slice(ct * LANE, (ct + 1) * LANE)
        bias = jnp.broadcast_to(cb_ref[:, ls], (SUBLANE, LANE))

        def tile(sh, idx, ct=ct):
            return ext_ref[sh, ct, pl.ds(pl.multiple_of(idx * SUBLANE, SUBLANE), SUBLANE), :]

        def body(k, carry, ls=ls, bias=bias, tile=tile):
            new_carry, partial = [], []
            for sh in range(SUBLANE):
                tiles = list(carry[sh]) + [tile(sh, k + first[sh] + depth[sh] - 1)]
                acc = None
                for j in taps[sh]:
                    term = cw_ref[j:j + 1, ls] * tiles[((pad + j) - sh) // SUBLANE - first[sh]]
                    acc = term if acc is None else acc + term
                partial.append(acc)
                new_carry.append(tuple(tiles[1:]))
            while len(partial) > 1:
                partial = [partial[i] + partial[i + 1] for i in range(0, len(partial), 2)]
            y_ref[pl.ds(pl.multiple_of(k * SUBLANE, SUBLANE), SUBLANE), ls] = partial[0] + bias
            return tuple(new_carry)

        init = tuple(tuple(tile(sh, first[sh] + a) for a in range(depth[sh] - 1)) for sh in range(SUBLANE))
        lax.fori_loop(0, tb // SUBLANE, body, init, unroll=min(CONV_UNROLL, tb // SUBLANE))

    y = y_ref[...]
    mu = jnp.mean(y, axis=-1, keepdims=True)
    d = y - mu
    var = jnp.mean(d * d, axis=-1, keepdims=True)
    yn = d * lax.rsqrt(var + LN_EPS) * lnw_ref[...] + lnb_ref[...]
    g = p_ref[:, 2 * CONV_C:3 * CONV_C]
    o_ref[...] = (yn * _sigmoid(yn)) * (g * _sigmoid(g))

    @pl.when(c == pl.num_programs(1) - 1)
    def _():
        cache_out_ref[...] = base_ref[pad + n_valid:pad + n_valid + CONV_HIST, :]

    base_ref[0:CONV_HIST_PAD, :] = base_ref[tb:rows, :]


def _odd_mix(p, layer, n_layers, cache, prev_out, cw, cb, lnw, lnb, tb, n_valid, nb, pos_major):
    bsz, lp, _ = p.shape
    nc = lp // tb
    const = lambda shape: pl.BlockSpec(shape, lambda b, c: (0,) * len(shape))
    if pos_major:
        cache_block, cache_shape = (None, CONV_HIST, nb, CONV_C), (n_layers, CONV_HIST, bsz, CONV_C)
        layer_map = lambda b, c: (layer, 0, b, 0)
    else:
        cache_block, cache_shape = (None, nb, CONV_HIST, CONV_C), (n_layers, bsz, CONV_HIST, CONV_C)
        layer_map = lambda b, c: (layer, b, 0, 0)
    cache_specs = [] if cache is None else [pl.BlockSpec(cache_block, layer_map)]
    cache_args = [] if cache is None else [cache]
    alias_specs = [] if prev_out is None else [pl.BlockSpec(memory_space=pl.ANY)]
    alias_args = [] if prev_out is None else [prev_out]
    aliases = {} if prev_out is None else {1 + len(cache_args) + 4: 1}
    return pl.pallas_call(
        functools.partial(_odd_mix_kernel, tb=tb, n_valid=n_valid, nb=nb, has_cache=cache is not None,
                          n_alias=len(alias_args), pos_major=pos_major),
        grid=(bsz // nb, nc),
        in_specs=[
            pl.BlockSpec((nb, tb, ODD_PROJ), lambda b, c: (b, c, 0)),
            *cache_specs,
            const((CONV_W, CONV_C)), const((1, CONV_C)), const((1, CONV_C)), const((1, CONV_C)),
            *alias_specs,
        ],
        out_specs=[
            pl.BlockSpec((nb, tb, CONV_C), lambda b, c: (b, c, 0)),
            pl.BlockSpec(cache_block, layer_map),
        ],
        out_shape=[
            jax.ShapeDtypeStruct((bsz, lp, CONV_C), f32),
            jax.ShapeDtypeStruct(cache_shape, f32),
        ],
        input_output_aliases=aliases,
        scratch_shapes=[pltpu.VMEM((nb, CONV_HIST_PAD + tb + SUBLANE, CONV_C), f32),
                        pltpu.VMEM((SUBLANE, CONV_C // LANE, CONV_HIST_PAD + tb, LANE), f32),
                        pltpu.VMEM((tb, CONV_C), f32)],
        compiler_params=pltpu.CompilerParams(
            dimension_semantics=("arbitrary", "arbitrary"), vmem_limit_bytes=VMEM_LIMIT_BYTES),
        name="odd_mix",
    )(p, *cache_args, cw, cb, lnw, lnb, *alias_args)


def _row(x):
    return x.reshape(1, -1)


def kernel(x_prompt, x_sample, state_rwkv_shift, state_rwkv, state_gla, cache_conv, norm_even, w_in_even, rwkv_mu, rwkv_w0, rwkv_w_up, rwkv_a0, rwkv_a_up, rwkv_k_k, rwkv_k_a, rwkv_r_k, rwkv_ln_w, rwkv_ln_b, gla_g_up, gla_g_b, gla_norm, w_out_even, norm_odd, w_in_odd, b_in_odd, conv_w, conv_b, conv_ln_w, conv_ln_b, w_out_odd, final_norm):
    bp, lp, _ = x_prompt.shape
    bs, ls, _ = x_sample.shape
    depth = norm_even.shape[0] + norm_odd.shape[0]
    xs_pad = jnp.pad(x_sample, ((0, 0), (0, SAMPLE_PAD_LEN - ls), (0, 0)))
    groups = [
        dict(x=x_prompt.reshape(bp * lp, D_MODEL), b=bp, l=lp, nv=lp, t=PROMPT_CHUNK, tb=CONV_BLOCK, prompt=True,
             nb=PROMPT_SEQS_PER_STEP, cnb=1),
        dict(x=xs_pad.reshape(bs * SAMPLE_PAD_LEN, D_MODEL), b=bs, l=SAMPLE_PAD_LEN, nv=ls, t=SAMPLE_PAD_LEN,
             tb=SAMPLE_PAD_LEN, prompt=False, nb=SAMPLE_SEQS_PER_STEP, cnb=SAMPLE_SEQS_PER_STEP),
    ]
    bd = (jnp.arange(LANE)[:, None] // RWKV_HEAD == jnp.arange(LANE)[None, :] // RWKV_HEAD).astype(bf16)
    zeros_lora = jnp.zeros((RWKV_LORA, RWKV_W), f32)
    outs = [dict(shift=[], states=None, conv=None) for _ in groups]
    n_even, n_odd = norm_even.shape[0], norm_odd.shape[0]

    proj = []
    for layer in range(depth):
        i = layer // 2
        if layer % 2 == 0:
            w = w_in_even[i]
            w_pad = jnp.concatenate(
                [w[:, :RWKV_PROJ + GLA_PROJ], jnp.zeros((D_MODEL, GLA_LORA_PAD - GLA_LORA), f32),
                 w[:, RWKV_PROJ + GLA_PROJ:]], axis=1).astype(bf16)
            proj.append((_row(norm_even[i]), w_pad, jnp.zeros((1, EVEN_PROJ_PAD), f32), w_out_even[i].astype(bf16)))
        else:
            proj.append((_row(norm_odd[i]), w_in_odd[i].astype(bf16), _row(b_in_odd[i]), w_out_odd[i].astype(bf16)))
    for g in groups:
        g["p"] = _inproj(g["x"], *proj[0][:3])

    def next_layer(g, layer, o2d):
        if layer == depth - 1:
            g["x"] = _outproj(o2d, g["x"], proj[layer][3], _row(final_norm))
        else:
            g["x"], g["p"] = _midproj(o2d, g["x"], proj[layer][3], *proj[layer + 1][:3])

    for layer in range(depth):
        i = layer // 2
        if layer % 2 == 0:
            wts = (
                _row(rwkv_mu[i]), _row(rwkv_w0[i]),
                _split_weight(jnp.concatenate([rwkv_w_up[i], zeros_lora], axis=0)),
                _row(rwkv_a0[i]), _split_weight(jnp.concatenate([zeros_lora, rwkv_a_up[i]], axis=0)),
                _row(rwkv_k_k[i]), _row(rwkv_k_a[i]), _row(rwkv_r_k[i]), _row(rwkv_ln_w[i]), _row(rwkv_ln_b[i]),
                _split_weight(jnp.concatenate([gla_g_up[i], jnp.zeros((GLA_LORA_PAD - GLA_LORA, GLA_KW), f32)],
                                              axis=0)),
                _row(gla_g_b[i]), _row(gla_norm[i]), bd,
            )
            for gi, g in enumerate(groups):
                p = g["p"].reshape(g["b"], g["l"], EVEN_PROJ_PAD)
                states = None if g["prompt"] else (state_rwkv_shift, state_rwkv, state_gla)
                og, sr, sg = _even_mix(p, i, n_even, states, outs[gi]["states"], wts, g["t"],
                                       min(g["nv"], g["t"]), g["nb"])
                outs[gi]["shift"].append(p[:, g["nv"] - 1, :RWKV_PROJ])
                outs[gi]["states"] = (sr, sg)
                next_layer(g, layer, og.reshape(-1, EVEN_WIDTH))
        else:
            for gi, g in enumerate(groups):
                p = g["p"].reshape(g["b"], g["l"], ODD_PROJ)
                cache = None if g["prompt"] else jnp.transpose(cache_conv, (0, 2, 1, 3))
                y, outs[gi]["conv"] = _odd_mix(p, i, n_odd, cache, outs[gi]["conv"], conv_w[i], _row(conv_b[i]),
                                               _row(conv_ln_w[i]), _row(conv_ln_b[i]), g["tb"],
                                               min(g["nv"], g["tb"]), g["cnb"], not g["prompt"])
                next_layer(g, layer, y.reshape(-1, CONV_C))

    y_prompt = groups[0]["x"].reshape(bp, lp, D_MODEL)
    y_sample = groups[1]["x"].reshape(bs, SAMPLE_PAD_LEN, D_MODEL)[:, :ls]
    op, os_ = outs
    return (y_prompt, y_sample,
            jnp.stack(op["shift"]), jnp.stack(os_["shift"]),
            op["states"][0], os_["states"][0],
            op["states"][1], os_["states"][1],
            op["conv"], jnp.transpose(os_["conv"], (0, 2, 1, 3)))
```

```python
import functools
import math

import jax
import jax.numpy as jnp
import numpy as np
from jax import lax
from jax.experimental import pallas as pl
from jax.experimental.pallas import tpu as pltpu

f32 = jnp.float32
bf16 = jnp.bfloat16

D_MODEL = 1024
RWKV_HEAD = 64
RWKV_HEADS = 8
RWKV_W = 512
RWKV_LORA = 64
RWKV_PROJ = 3 * RWKV_W + 2 * RWKV_LORA
GLA_HEADS = 4
GLA_DK = 64
GLA_DV = 128
GLA_KW = GLA_HEADS * GLA_DK
GLA_VW = GLA_HEADS * GLA_DV
GLA_LORA = 16
GLA_LORA_PAD = 128
GLA_PROJ = 2 * GLA_KW + GLA_VW + GLA_LORA
EVEN_WIDTH = RWKV_W + GLA_VW
Q_OFF = RWKV_PROJ
K_OFF = Q_OFF + GLA_KW
V_OFF = K_OFF + GLA_KW
GD_OFF = V_OFF + GLA_VW
GATE_OFF = GD_OFF + GLA_LORA_PAD
EVEN_PROJ_PAD = GATE_OFF + EVEN_WIDTH
CONV_C = 1024
CONV_W = 31
CONV_HIST = CONV_W - 1
CONV_HIST_PAD = 32
ODD_PROJ = 3 * CONV_C

RMS_EPS = 1e-6
LN_EPS = 1e-5
RWKV_GN_EPS = 64e-5
GLA_GATE_NORM = 16.0
PROMPT_CHUNK = 64
SAMPLE_PAD_LEN = 8
PROMPT_SEQS_PER_STEP = 8
SAMPLE_SEQS_PER_STEP = 16
CONV_BLOCK = 256
CONV_UNROLL = 8
LANE = 128
SUBLANE = 8
BF16_SUBLANES = 16
HEAD_GROUP = 2
VMEM_LIMIT_BYTES = 48 * 1024 * 1024

NN = ((1,), (0,))
NT = ((1,), (1,))
TN = ((0,), (0,))


def _dg(a, b, dims=NN):
    return lax.dot_general(a, b, (dims, ((), ())), preferred_element_type=f32)


def _dot1(a, b, dims=NN):
    return _dg(a.astype(bf16), b.astype(bf16), dims)


def _split2(x):
    hi = x.astype(bf16)
    lo = (x - hi.astype(f32)).astype(bf16)
    return hi, lo


def _split3(x):
    hi = x.astype(bf16)
    r1 = x - hi.astype(f32)
    mid = r1.astype(bf16)
    lo = (r1 - mid.astype(f32)).astype(bf16)
    return hi, mid, lo


def _dot3(a, b_ref):
    ah, al = _split2(a)
    bh, bl = b_ref[0], b_ref[1]
    return _dg(ah, bh) + _dg(ah, bl) + _dg(al, bh)


def _split_weight(w):
    hi = w.astype(bf16)
    return jnp.stack([hi, (w - hi.astype(f32)).astype(bf16)])


def _dot_exact_lhs(m, x):
    hi, mid, lo = _split3(x)
    return _dg(m, hi) + _dg(m, mid) + _dg(m, lo)


def _sigmoid(x):
    return 0.5 * jnp.tanh(0.5 * x) + 0.5


def _log_sigmoid(x):
    return jnp.minimum(x, 0.0) - jnp.log(1.0 + jnp.exp(-jnp.abs(x)))


def _rmsnorm(x, g):
    return x * lax.rsqrt(jnp.mean(x * x, axis=-1, keepdims=True) + RMS_EPS) * g


def _inproj_kernel(x_ref, g_ref, w_ref, b_ref, o_ref):
    h = _rmsnorm(x_ref[...], g_ref[...])
    o_ref[...] = _dg(h.astype(bf16), w_ref[...]) + b_ref[...]


def _inproj(x2d, g, w, b):
    m, n = x2d.shape[0], w.shape[1]
    tm = 512
    return pl.pallas_call(
        _inproj_kernel,
        grid=(m // tm,),
        in_specs=[
            pl.BlockSpec((tm, D_MODEL), lambda i: (i, 0)),
            pl.BlockSpec((1, D_MODEL), lambda i: (0, 0)),
            pl.BlockSpec((D_MODEL, n), lambda i: (0, 0)),
            pl.BlockSpec((1, n), lambda i: (0, 0)),
        ],
        out_specs=pl.BlockSpec((tm, n), lambda i: (i, 0)),
        out_shape=jax.ShapeDtypeStruct((m, n), f32),
        compiler_params=pltpu.CompilerParams(
            dimension_semantics=("arbitrary",), vmem_limit_bytes=VMEM_LIMIT_BYTES),
        name="inproj",
    )(x2d, g, w, b)


def _outproj_kernel(o_ref, x_ref, w_ref, fn_ref, y_ref):
    y_ref[...] = _rmsnorm(x_ref[...] + _dg(o_ref[...].astype(bf16), w_ref[...]), fn_ref[...])


def _outproj(o2d, x2d, w, fn):
    m = x2d.shape[0]
    tm = 512
    return pl.pallas_call(
        _outproj_kernel,
        grid=(m // tm,),
        in_specs=[
            pl.BlockSpec((tm, D_MODEL), lambda i: (i, 0)),
            pl.BlockSpec((tm, D_MODEL), lambda i: (i, 0)),
            pl.BlockSpec((D_MODEL, D_MODEL), lambda i: (0, 0)),
            pl.BlockSpec((1, D_MODEL), lambda i: (0, 0)),
        ],
        out_specs=pl.BlockSpec((tm, D_MODEL), lambda i: (i, 0)),
        out_shape=jax.ShapeDtypeStruct((m, D_MODEL), f32),
        compiler_params=pltpu.CompilerParams(
            dimension_semantics=("arbitrary",), vmem_limit_bytes=VMEM_LIMIT_BYTES),
        name="outproj",
    )(o2d, x2d, w, fn)


def _midproj_kernel(o_ref, x_ref, wo_ref, g_ref, wi_ref, b_ref, xn_ref, p_ref):
    y = x_ref[...] + _dg(o_ref[...].astype(bf16), wo_ref[...])
    xn_ref[...] = y
    p_ref[...] = _dg(_rmsnorm(y, g_ref[...]).astype(bf16), wi_ref[...]) + b_ref[...]


def _midproj(o2d, x2d, w_out, g, w_in, b):
    m, n = x2d.shape[0], w_in.shape[1]
    tm = 512
    row_block = pl.BlockSpec((tm, D_MODEL), lambda i: (i, 0))
    resident = lambda shape: pl.BlockSpec(shape, lambda i: (0, 0), pipeline_mode=pl.Buffered(1))
    return pl.pallas_call(
        _midproj_kernel,
        grid=(m // tm,),
        in_specs=[
            row_block, row_block,
            resident((D_MODEL, D_MODEL)), resident((1, D_MODEL)), resident((D_MODEL, n)), resident((1, n)),
        ],
        out_specs=[row_block, pl.BlockSpec((tm, n), lambda i: (i, 0))],
        out_shape=[jax.ShapeDtypeStruct((m, D_MODEL), f32), jax.ShapeDtypeStruct((m, n), f32)],
        compiler_params=pltpu.CompilerParams(
            dimension_semantics=("arbitrary",), vmem_limit_bytes=VMEM_LIMIT_BYTES),
        name="midproj",
    )(o2d, x2d, w_out, g, w_in, b)


N_EVEN_WEIGHTS = 16


def _even_mix_kernel(*refs, t, n_valid, nb, has_state, n_alias):
    p_ref = refs[0]
    n_in = 1 + (3 if has_state else 0)
    (mu_ref, w0_ref, wup_ref, a0_ref, aup_ref, kk_ref, ka_ref, rk_ref, lnw_ref, lnb_ref, gup_ref, gb_ref, gn_ref,
     bd_ref, fmask_ref, tmask_ref) = refs[n_in:n_in + N_EVEN_WEIGHTS]
    og_ref, sr_ref, sg_ref, prev_ref, srs_ref = refs[n_in + N_EVEN_WEIGHTS + n_alias:]
    c = pl.program_id(1)
    n_groups = RWKV_HEADS // HEAD_GROUP

    @pl.when(c == 0)
    def _():
        if not has_state:
            sg_ref[...] = jnp.zeros(sg_ref.shape, f32)
            prev_ref[...] = jnp.zeros(prev_ref.shape, f32)
            srs_ref[...] = jnp.zeros(srs_ref.shape, f32)
            return
        shift_ref, sr_in_ref, sg_in_ref = refs[1:4]
        sg_ref[...] = sg_in_ref[...]
        prev_ref[...] = shift_ref[...]
        for i in range(nb):
            for q in range(n_groups):
                rows_ = []
                for j in range(HEAD_GROUP):
                    parts = [jnp.zeros((RWKV_HEAD, RWKV_HEAD), f32)] * HEAD_GROUP
                    parts[j] = sr_in_ref[i, q * HEAD_GROUP + j]
                    rows_.append(jnp.concatenate(parts, axis=1))
                srs_ref[i, q] = jnp.concatenate(rows_, axis=0)

    row = lax.broadcasted_iota(jnp.int32, (t, t), 0)
    col = lax.broadcasted_iota(jnp.int32, (t, t), 1)
    incl = row >= col
    tril = jnp.where(incl, 1.0, 0.0).astype(bf16)
    rowv = lax.broadcasted_iota(jnp.int32, (t, 1), 0)
    valid = None if n_valid == t else rowv < n_valid
    bd = bd_ref[...]
    rk64 = lax.broadcasted_iota(jnp.int32, (GLA_DK, GLA_DK), 0)
    ck64 = lax.broadcasted_iota(jnp.int32, (GLA_DK, GLA_DK), 1)

    def head_sum(x):
        tiles = x.shape[1] // LANE
        stacked = jnp.concatenate([x[:, m * LANE:(m + 1) * LANE] for m in range(tiles)], axis=0)
        hi, lo = _split2(stacked)
        s = _dg(hi, bd) + _dg(lo, bd)
        return jnp.concatenate([s[m * t:(m + 1) * t] for m in range(tiles)], axis=1)

    gpairs = [(i, h) for i in range(nb) for h in range(GLA_HEADS)]

    seq = []
    for i in range(nb):
        pr = p_ref[i, :, 0:RWKV_PROJ]
        p_prev = jnp.where(rowv == 0, prev_ref[i], pltpu.roll(pr, 1, 0))
        prev_ref[i] = pr[t - 1:t, :]
        xm = pr + (p_prev - pr) * mu_ref[...]
        r = xm[:, 0:RWKV_W]
        k = xm[:, RWKV_W:2 * RWKV_W]
        v = xm[:, 2 * RWKV_W:3 * RWKV_W]
        xwa = xm[:, 3 * RWKV_W:RWKV_PROJ]
        wl = w0_ref[...] + _dot3(jnp.tanh(xwa), wup_ref)
        al = a0_ref[...] + _dot3(xwa, aup_ref)
        ld = (-math.exp(-0.5)) * _sigmoid(wl)
        a = _sigmoid(al)
        kkr = k * kk_ref[...]
        kkn = kkr * lax.rsqrt(jnp.maximum(head_sum(kkr * kkr), 1e-24))
        k2 = k * (1.0 + (a - 1.0) * ka_ref[...])
        if valid is not None:
            ld = jnp.where(valid, ld, 0.0)
            kkn = jnp.where(valid, kkn, 0.0)
            k2 = jnp.where(valid, k2, 0.0)
        c_in = _dot_exact_lhs(tril, ld)
        g_in = jnp.exp(c_in)
        ginv = jnp.exp(-c_in)
        seq.append(dict(r=r, k2=k2, v=v, g_last=g_in[t - 1:t, :],
                        abar=-kkn * jnp.exp(c_in - ld), bbar=kkn * a * ginv, kbar=k2 * ginv, rbar=r * g_in))

    fw = HEAD_GROUP * RWKV_HEAD
    tw = HEAD_GROUP * t
    rt = lax.broadcasted_iota(jnp.int32, (t, tw), 0)
    st = lax.broadcasted_iota(jnp.int32, (t, tw), 1) % t
    strict_g = rt > st
    incl_g = rt >= st
    eye_g = jnp.where(rt == st, 1.0, 0.0).astype(f32)
    first_level_g = (rt // 2 == st // 2) & (rt % 2 == 1) & (st % 2 == 0)
    rs = lax.broadcasted_iota(jnp.int32, (fw, fw), 0) // RWKV_HEAD
    cs = lax.broadcasted_iota(jnp.int32, (fw, fw), 1) // RWKV_HEAD
    state_diag = rs == cs

    def block_diag(x, mask):
        if x.shape[0] % BF16_SUBLANES == 0:
            xb = x.astype(bf16)
            return jnp.concatenate([xb] * HEAD_GROUP, axis=0) * mask
        return (jnp.concatenate([x] * HEAD_GROUP, axis=0) * mask.astype(f32)).astype(bf16)

    fmask = fmask_ref[...]
    groups = [(i, q) for i in range(nb) for q in range(RWKV_HEADS // HEAD_GROUP)]
    fs = lambda q: slice(q * fw, (q + 1) * fw)
    lm = {(i, q): jnp.concatenate([seq[i]["abar"][:, fs(q)], seq[i]["rbar"][:, fs(q)]], axis=0).astype(bf16)
          for i, q in groups}
    vq = {(i, q): seq[i]["v"][:, fs(q)] for i, q in groups}
    vbd = {iq: block_diag(vq[iq], fmask) for iq in groups}
    gb = {(i, q): _dg(lm[i, q], block_diag(seq[i]["bbar"][:, fs(q)], fmask), NT) for i, q in groups}
    gk = {(i, q): _dg(lm[i, q], block_diag(seq[i]["kbar"][:, fs(q)], fmask), NT) for i, q in groups}
    a_ab = {iq: jnp.where(strict_g, gb[iq][:t], 0.0) for iq in groups}
    a_ak = {iq: jnp.where(strict_g, gk[iq][:t], 0.0).astype(bf16) for iq in groups}
    a_rb = {iq: jnp.where(incl_g, gb[iq][t:], 0.0).astype(bf16) for iq in groups}
    a_rk = {iq: jnp.where(incl_g, gk[iq][t:], 0.0).astype(bf16) for iq in groups}
    minv = {iq: eye_g + jnp.where(first_level_g, a_ab[iq], 0.0) for iq in groups}
    for lvl in range(1, tmask_ref.shape[0]):
        tmp = {iq: _dg(minv[iq].astype(bf16), block_diag(a_ab[iq], tmask_ref[lvl])) for iq in groups}
        minv = {iq: minv[iq] + _dg(tmp[iq].astype(bf16), block_diag(minv[iq], tmask_ref[0])) for iq in groups}
    akv = {iq: _dg(a_ak[iq], vbd[iq]) for iq in groups}
    s_old = {(i, q): srs_ref[i, q] for i, q in groups}
    ls = {iq: _dg(lm[iq], s_old[iq].astype(bf16), NT) for iq in groups}
    z = {iq: _dg(minv[iq].astype(bf16), block_diag(ls[iq][:t] + akv[iq], fmask)) for iq in groups}
    yq = {iq: ls[iq][t:] + _dg(a_rb[iq], block_diag(z[iq], fmask)) + _dg(a_rk[iq], vbd[iq]) for iq in groups}
    for i, q in groups:
        zv = jnp.concatenate([z[i, q], vq[i, q]], axis=0).astype(bf16)
        rm = jnp.concatenate([seq[i]["bbar"][:, fs(q)], seq[i]["kbar"][:, fs(q)]], axis=0).astype(bf16)
        upd = jnp.where(state_diag, _dg(zv, rm, TN), 0.0)
        srs_ref[i, q] = (s_old[i, q] + upd) * seq[i]["g_last"][:, fs(q)]

    gseq = []
    for i in range(nb):
        q = p_ref[i, :, Q_OFF:K_OFF]
        kg = p_ref[i, :, K_OFF:V_OFF]
        gd = p_ref[i, :, GD_OFF:GATE_OFF]
        la = _log_sigmoid(_dot3(gd, gup_ref) + gb_ref[...]) * (1.0 / GLA_GATE_NORM)
        if valid is not None:
            la = jnp.where(valid, la, 0.0)
            kg = jnp.where(valid, kg, 0.0)
        b = _dot_exact_lhs(tril, la)
        b_last = b[t - 1:t, :]
        gseq.append(dict(qg=(q * jnp.exp(b) * (GLA_DK ** -0.5)).astype(bf16), kgi=(kg * jnp.exp(-b)).astype(bf16),
                         kd=(kg * jnp.exp(b_last - b)).astype(bf16), eb=jnp.exp(b_last)))
    gs = lambda h: slice(h * GLA_DK, (h + 1) * GLA_DK)
    gv = {(i, h): p_ref[i, :, V_OFF + h * GLA_DV:V_OFF + (h + 1) * GLA_DV].astype(bf16) for i, h in gpairs}
    att = {(i, h): jnp.where(incl, _dg(gseq[i]["qg"][:, gs(h)], gseq[i]["kgi"][:, gs(h)], NT), 0.0).astype(bf16)
           for i, h in gpairs}
    gs_old = {(i, h): sg_ref[i, h] for i, h in gpairs}
    go = {(i, h): _dg(att[i, h], gv[i, h]) + _dg(gseq[i]["qg"][:, gs(h)], gs_old[i, h].astype(bf16))
          for i, h in gpairs}
    for i, h in gpairs:
        e_col = jnp.sum(jnp.where(rk64 == ck64, gseq[i]["eb"][:, gs(h)], 0.0), axis=1, keepdims=True)
        sg_ref[i, h] = e_col * gs_old[i, h] + _dg(gseq[i]["kd"][:, gs(h)], gv[i, h], TN)

    inv_k = 1.0 / RWKV_HEAD
    for i in range(nb):
        y = jnp.concatenate([yq[i, q] for q in range(n_groups)], axis=1)
        mean = head_sum(y) * inv_k
        dlt = y - mean
        var = head_sum(dlt * dlt) * inv_k
        yn = dlt * lax.rsqrt(var + RWKV_GN_EPS) * lnw_ref[...] + lnb_ref[...]
        o_r = yn + head_sum(seq[i]["r"] * seq[i]["k2"] * rk_ref[...]) * seq[i]["v"]
        o_g = jnp.concatenate([_rmsnorm(go[i, h], gn_ref[...]) for h in range(GLA_HEADS)], axis=1)
        gate = p_ref[i, :, GATE_OFF:EVEN_PROJ_PAD]
        og_ref[i] = jnp.concatenate([o_r, o_g], axis=1) * (gate * _sigmoid(gate))

    @pl.when(c == pl.num_programs(1) - 1)
    def _():
        for i in range(nb):
            for q in range(n_groups):
                for j in range(HEAD_GROUP):
                    lo = j * RWKV_HEAD
                    sr_ref[i, q * HEAD_GROUP + j] = srs_ref[i, q, lo:lo + RWKV_HEAD, lo:lo + RWKV_HEAD]


def _group_masks(t):
    rows = np.arange(HEAD_GROUP * t)[:, None]
    fmask = rows // t == np.arange(HEAD_GROUP * RWKV_HEAD)[None, :] // RWKV_HEAD
    lanes = np.arange(HEAD_GROUP * t)[None, :]
    same_head = rows // t == lanes // t
    r, s = rows % t, lanes % t
    levels = [same_head]
    blk = 2
    while blk < t:
        levels.append(same_head & (r // (2 * blk) == s // (2 * blk)) & ((r // blk) % 2 == 1) & ((s // blk) % 2 == 0))
        blk *= 2
    return jnp.asarray(fmask, bf16), jnp.asarray(np.stack(levels), bf16)


def _even_mix(p, layer, n_layers, states, prev_outs, wts, t, n_valid, nb):
    bsz, lp, _ = p.shape
    nc = lp // t
    const = lambda shape: pl.BlockSpec(shape, lambda b, c: (0,) * len(shape))
    fmask, tmask = _group_masks(t)
    n_groups = RWKV_HEADS // HEAD_GROUP
    slab = HEAD_GROUP * RWKV_HEAD
    sr_block = (None, nb, RWKV_HEADS, RWKV_HEAD, RWKV_HEAD)
    sg_block = (None, nb, GLA_HEADS, GLA_DK, GLA_DV)
    layer_map = lambda b, c: (layer, b, 0, 0, 0)
    state_specs, state_args = [], []
    if states is not None:
        shift, sr0, sg0 = states
        state_specs = [pl.BlockSpec((None, nb, 1, RWKV_PROJ), lambda b, c: (layer, b, 0, 0)),
                       pl.BlockSpec(sr_block, layer_map), pl.BlockSpec(sg_block, layer_map)]
        state_args = [shift.reshape(shift.shape[0], bsz, 1, RWKV_PROJ), sr0, sg0]
    alias_specs, alias_args, aliases = [], [], {}
    if prev_outs is not None:
        alias_specs = [pl.BlockSpec(memory_space=pl.ANY)] * 2
        alias_args = list(prev_outs)
        first = 1 + len(state_args) + N_EVEN_WEIGHTS
        aliases = {first: 1, first + 1: 2}
    return pl.pallas_call(
        functools.partial(_even_mix_kernel, t=t, n_valid=n_valid, nb=nb, has_state=states is not None,
                          n_alias=len(alias_args)),
        grid=(bsz // nb, nc),
        in_specs=[
            pl.BlockSpec((nb, t, EVEN_PROJ_PAD), lambda b, c: (b, c, 0)),
            *state_specs,
            const((1, RWKV_PROJ)),
            const((1, RWKV_W)), const((2, 2 * RWKV_LORA, RWKV_W)),
            const((1, RWKV_W)), const((2, 2 * RWKV_LORA, RWKV_W)),
            const((1, RWKV_W)), const((1, RWKV_W)), const((1, RWKV_W)),
            const((1, RWKV_W)), const((1, RWKV_W)),
            const((2, GLA_LORA_PAD, GLA_KW)), const((1, GLA_KW)), const((1, GLA_DV)),
            const((LANE, LANE)), const(fmask.shape), const(tmask.shape),
            *alias_specs,
        ],
        out_specs=[
            pl.BlockSpec((nb, t, EVEN_WIDTH), lambda b, c: (b, c, 0)),
            pl.BlockSpec(sr_block, layer_map),
            pl.BlockSpec(sg_block, layer_map),
        ],
        out_shape=[
            jax.ShapeDtypeStruct((bsz, lp, EVEN_WIDTH), f32),
            jax.ShapeDtypeStruct((n_layers, bsz, RWKV_HEADS, RWKV_HEAD, RWKV_HEAD), f32),
            jax.ShapeDtypeStruct((n_layers, bsz, GLA_HEADS, GLA_DK, GLA_DV), f32),
        ],
        input_output_aliases=aliases,
        scratch_shapes=[pltpu.VMEM((nb, 1, RWKV_PROJ), f32), pltpu.VMEM((nb, n_groups, slab, slab), f32)],
        compiler_params=pltpu.CompilerParams(
            dimension_semantics=("arbitrary", "arbitrary"), vmem_limit_bytes=VMEM_LIMIT_BYTES),
        name="even_mix",
    )(p, *state_args, *wts, fmask, tmask, *alias_args)


def _odd_mix_kernel(*refs, tb, n_valid, nb, has_cache, n_alias, pos_major):
    p_ref = refs[0]
    n_in = 2 if has_cache else 1
    cw_ref, cb_ref, lnw_ref, lnb_ref = refs[n_in:n_in + 4]
    o_ref, cache_out_ref, bases_ref, ext_ref, y_ref = refs[n_in + 4 + n_alias:]
    seq_view = (lambda ref, i: ref.at[:, i]) if pos_major else (lambda ref, i: ref.at[i])
    for i in range(nb):
        _odd_mix_one(p_ref.at[i], seq_view(refs[1], i) if has_cache else None, cw_ref, cb_ref, lnw_ref, lnb_ref,
                     o_ref.at[i], seq_view(cache_out_ref, i), bases_ref.at[i], ext_ref, y_ref,
                     tb=tb, n_valid=n_valid)


def _odd_mix_one(p_ref, cache_ref, cw_ref, cb_ref, lnw_ref, lnb_ref, o_ref, cache_out_ref, base_ref, ext_ref,
                 y_ref, *, tb, n_valid):
    c = pl.program_id(1)
    pad = CONV_HIST_PAD - CONV_HIST
    rows = CONV_HIST_PAD + tb

    @pl.when(c == 0)
    def _():
        if cache_ref is not None:
            base_ref[0:SUBLANE, :] = jnp.zeros((SUBLANE, CONV_C), f32)
            base_ref[pad:CONV_HIST_PAD, :] = cache_ref[...]
        else:
            base_ref[0:CONV_HIST_PAD, :] = jnp.zeros((CONV_HIST_PAD, CONV_C), f32)
        base_ref[rows:rows + SUBLANE, :] = jnp.zeros((SUBLANE, CONV_C), f32)

    base_ref[CONV_HIST_PAD:rows, :] = p_ref[:, 0:CONV_C] * _sigmoid(p_ref[:, CONV_C:2 * CONV_C])
    for sh in range(SUBLANE):
        for ct in range(CONV_C // LANE):
            ext_ref[sh, ct] = base_ref[sh:sh + rows, ct * LANE:(ct + 1) * LANE]

    taps = [[j for j in range(CONV_W) if (pad + j) % SUBLANE == sh] for sh in range(SUBLANE)]
    first = [((pad + taps[sh][0]) - sh) // SUBLANE for sh in range(SUBLANE)]
    depth = [((pad + taps[sh][-1]) - sh) // SUBLANE - first[sh] + 1 for sh in range(SUBLANE)]
    for ct in range(CONV_C // LANE):
        ls = slice(ct * LANE, (ct + 1) * LANE)
        bias = jnp.broadcast_to(cb_ref[:, ls], (SUBLANE, LANE))

        def tile(sh, idx, ct=ct):
            return ext_ref[sh, ct, pl.ds(pl.multiple_of(idx * SUBLANE, SUBLANE), SUBLANE), :]

        def body(k, carry, ls=ls, bias=bias, tile=tile):
            new_carry, partial = [], []
            for sh in range(SUBLANE):
                tiles = list(carry[sh]) + [tile(sh, k + first[sh] + depth[sh] - 1)]
                acc = None
                for j in taps[sh]:
                    term = cw_ref[j:j + 1, ls] * tiles[((pad + j) - sh) // SUBLANE - first[sh]]
                    acc = term if acc is None else acc + term
                partial.append(acc)
                new_carry.append(tuple(tiles[1:]))
            while len(partial) > 1:
                partial = [partial[i] + partial[i + 1] for i in range(0, len(partial), 2)]
            y_ref[pl.ds(pl.multiple_of(k * SUBLANE, SUBLANE), SUBLANE), ls] = partial[0] + bias
            return tuple(new_carry)

        init = tuple(tuple(tile(sh, first[sh] + a) for a in range(depth[sh] - 1)) for sh in range(SUBLANE))
        lax.fori_loop(0, tb // SUBLANE, body, init, unroll=min(CONV_UNROLL, tb // SUBLANE))

    y = y_ref[...]
    mu = jnp.mean(y, axis=-1, keepdims=True)
    d = y - mu
    var = jnp.mean(d * d, axis=-1, keepdims=True)
    yn = d * lax.rsqrt(var + LN_EPS) * lnw_ref[...] + lnb_ref[...]
    g = p_ref[:, 2 * CONV_C:3 * CONV_C]
    o_ref[...] = (yn * _sigmoid(yn)) * (g * _sigmoid(g))

    @pl.when(c == pl.num_programs(1) - 1)
    def _():
        cache_out_ref[...] = base_ref[pad + n_valid:pad + n_valid + CONV_HIST, :]

    base_ref[0:CONV_HIST_PAD, :] = base_ref[tb:rows, :]


def _odd_mix(p, layer, n_layers, cache, prev_out, cw, cb, lnw, lnb, tb, n_valid, nb, pos_major):
    bsz, lp, _ = p.shape
    nc = lp // tb
    const = lambda shape: pl.BlockSpec(shape, lambda b, c: (0,) * len(shape))
    if pos_major:
        cache_block, cache_shape = (None, CONV_HIST, nb, CONV_C), (n_layers, CONV_HIST, bsz, CONV_C)
        layer_map = lambda b, c: (layer, 0, b, 0)
    else:
        cache_block, cache_shape = (None, nb, CONV_HIST, CONV_C), (n_layers, bsz, CONV_HIST, CONV_C)
        layer_map = lambda b, c: (layer, b, 0, 0)
    cache_specs = [] if cache is None else [pl.BlockSpec(cache_block, layer_map)]
    cache_args = [] if cache is None else [cache]
    alias_specs = [] if prev_out is None else [pl.BlockSpec(memory_space=pl.ANY)]
    alias_args = [] if prev_out is None else [prev_out]
    aliases = {} if prev_out is None else {1 + len(cache_args) + 4: 1}
    return pl.pallas_call(
        functools.partial(_odd_mix_kernel, tb=tb, n_valid=n_valid, nb=nb, has_cache=cache is not None,
                          n_alias=len(alias_args), pos_major=pos_major),
        grid=(bsz // nb, nc),
        in_specs=[
            pl.BlockSpec((nb, tb, ODD_PROJ), lambda b, c: (b, c, 0)),
            *cache_specs,
            const((CONV_W, CONV_C)), const((1, CONV_C)), const((1, CONV_C)), const((1, CONV_C)),
            *alias_specs,
        ],
        out_specs=[
            pl.BlockSpec((nb, tb, CONV_C), lambda b, c: (b, c, 0)),
            pl.BlockSpec(cache_block, layer_map),
        ],
        out_shape=[
            jax.ShapeDtypeStruct((bsz, lp, CONV_C), f32),
            jax.ShapeDtypeStruct(cache_shape, f32),
        ],
        input_output_aliases=aliases,
        scratch_shapes=[pltpu.VMEM((nb, CONV_HIST_PAD + tb + SUBLANE, CONV_C), f32),
                        pltpu.VMEM((SUBLANE, CONV_C // LANE, CONV_HIST_PAD + tb, LANE), f32),
                        pltpu.VMEM((tb, CONV_C), f32)],
        compiler_params=pltpu.CompilerParams(
            dimension_semantics=("arbitrary", "arbitrary"), vmem_limit_bytes=VMEM_LIMIT_BYTES),
        name="odd_mix",
    )(p, *cache_args, cw, cb, lnw, lnb, *alias_args)


def _row(x):
    return x.reshape(1, -1)


def kernel(x_prompt, x_sample, state_rwkv_shift, state_rwkv, state_gla, cache_conv, norm_even, w_in_even, rwkv_mu, rwkv_w0, rwkv_w_up, rwkv_a0, rwkv_a_up, rwkv_k_k, rwkv_k_a, rwkv_r_k, rwkv_ln_w, rwkv_ln_b, gla_g_up, gla_g_b, gla_norm, w_out_even, norm_odd, w_in_odd, b_in_odd, conv_w, conv_b, conv_ln_w, conv_ln_b, w_out_odd, final_norm):
    bp, lp, _ = x_prompt.shape
    bs, ls, _ = x_sample.shape
    depth = norm_even.shape[0] + norm_odd.shape[0]
    xs_pad = jnp.pad(x_sample, ((0, 0), (0, SAMPLE_PAD_LEN - ls), (0, 0)))
    groups = [
        dict(x=x_prompt.reshape(bp * lp, D_MODEL), b=bp, l=lp, nv=lp, t=PROMPT_CHUNK, tb=CONV_BLOCK, prompt=True,
             nb=PROMPT_SEQS_PER_STEP, cnb=1),
        dict(x=xs_pad.reshape(bs * SAMPLE_PAD_LEN, D_MODEL), b=bs, l=SAMPLE_PAD_LEN, nv=ls, t=SAMPLE_PAD_LEN,
             tb=SAMPLE_PAD_LEN, prompt=False, nb=SAMPLE_SEQS_PER_STEP, cnb=SAMPLE_SEQS_PER_STEP),
    ]
    bd = (jnp.arange(LANE)[:, None] // RWKV_HEAD == jnp.arange(LANE)[None, :] // RWKV_HEAD).astype(bf16)
    zeros_lora = jnp.zeros((RWKV_LORA, RWKV_W), f32)
    outs = [dict(shift=[], states=None, conv=None) for _ in groups]
    n_even, n_odd = norm_even.shape[0], norm_odd.shape[0]

    proj = []
    for layer in range(depth):
        i = layer // 2
        if layer % 2 == 0:
            w = w_in_even[i]
            w_pad = jnp.concatenate(
                [w[:, :RWKV_PROJ + GLA_PROJ], jnp.zeros((D_MODEL, GLA_LORA_PAD - GLA_LORA), f32),
                 w[:, RWKV_PROJ + GLA_PROJ:]], axis=1).astype(bf16)
            proj.append((_row(norm_even[i]), w_pad, jnp.zeros((1, EVEN_PROJ_PAD), f32), w_out_even[i].astype(bf16)))
        else:
            proj.append((_row(norm_odd[i]), w_in_odd[i].astype(bf16), _row(b_in_odd[i]), w_out_odd[i].astype(bf16)))
    for g in groups:
        g["p"] = _inproj(g["x"], *proj[0][:3])

    def next_layer(g, layer, o2d):
        if layer == depth - 1:
            g["x"] = _outproj(o2d, g["x"], proj[layer][3], _row(final_norm))
        else:
            g["x"], g["p"] = _midproj(o2d, g["x"], proj[layer][3], *proj[layer + 1][:3])

    for layer in range(depth):
        i = layer // 2
        if layer % 2 == 0:
            wts = (
                _row(rwkv_mu[i]), _row(rwkv_w0[i]),
                _split_weight(jnp.concatenate([rwkv_w_up[i], zeros_lora], axis=0)),
                _row(rwkv_a0[i]), _split_weight(jnp.concatenate([zeros_lora, rwkv_a_up[i]], axis=0)),
                _row(rwkv_k_k[i]), _row(rwkv_k_a[i]), _row(rwkv_r_k[i]), _row(rwkv_ln_w[i]), _row(rwkv_ln_b[i]),
                _split_weight(jnp.concatenate([gla_g_up[i], jnp.zeros((GLA_LORA_PAD - GLA_LORA, GLA_KW), f32)],
                                              axis=0)),
                _row(gla_g_b[i]), _row(gla_norm[i]), bd,
            )
            for gi, g in enumerate(groups):
                p = g["p"].reshape(g["b"], g["l"], EVEN_PROJ_PAD)
                states = None if g["prompt"] else (state_rwkv_shift, state_rwkv, state_gla)
                og, sr, sg = _even_mix(p, i, n_even, states, outs[gi]["states"], wts, g["t"],
                                       min(g["nv"], g["t"]), g["nb"])
                outs[gi]["shift"].append(p[:, g["nv"] - 1, :RWKV_PROJ])
                outs[gi]["states"] = (sr, sg)
                next_layer(g, layer, og.reshape(-1, EVEN_WIDTH))
        else:
            for gi, g in enumerate(groups):
                p = g["p"].reshape(g["b"], g["l"], ODD_PROJ)
                cache = None if g["prompt"] else jnp.transpose(cache_conv, (0, 2, 1, 3))
                y, outs[gi]["conv"] = _odd_mix(p, i, n_odd, cache, outs[gi]["conv"], conv_w[i], _row(conv_b[i]),
                                               _row(conv_ln_w[i]), _row(conv_ln_b[i]), g["tb"],
                                               min(g["nv"], g["tb"]), g["cnb"], not g["prompt"])
                next_layer(g, layer, y.reshape(-1, CONV_C))

    y_prompt = groups[0]["x"].reshape(bp, lp, D_MODEL)
    y_sample = groups[1]["x"].reshape(bs, SAMPLE_PAD_LEN, D_MODEL)[:, :ls]
    op, os_ = outs
    return (y_prompt, y_sample,
            jnp.stack(op["shift"]), jnp.stack(os_["shift"]),
            op["states"][0], os_["states"][0],
            op["states"][1], os_["states"][1],
            op["conv"], jnp.transpose(os_["conv"], (0, 2, 1, 3)))
```

```python
import functools
import math

import jax
import jax.numpy as jnp
import numpy as np
from jax import lax
from jax.experimental import pallas as pl
from jax.experimental.pallas import tpu as pltpu

f32 = jnp.float32
bf16 = jnp.bfloat16

D_MODEL = 1024
RWKV_HEAD = 64
RWKV_HEADS = 8
RWKV_W = 512
RWKV_LORA = 64
RWKV_PROJ = 3 * RWKV_W + 2 * RWKV_LORA
GLA_HEADS = 4
GLA_DK = 64
GLA_DV = 128
GLA_KW = GLA_HEADS * GLA_DK
GLA_VW = GLA_HEADS * GLA_DV
GLA_LORA = 16
GLA_LORA_PAD = 128
GLA_PROJ = 2 * GLA_KW + GLA_VW + GLA_LORA
EVEN_WIDTH = RWKV_W + GLA_VW
Q_OFF = RWKV_PROJ
K_OFF = Q_OFF + GLA_KW
V_OFF = K_OFF + GLA_KW
GD_OFF = V_OFF + GLA_VW
GATE_OFF = GD_OFF + GLA_LORA_PAD
EVEN_PROJ_PAD = GATE_OFF + EVEN_WIDTH
CONV_C = 1024
CONV_W = 31
CONV_HIST = CONV_W - 1
CONV_HIST_PAD = 32
ODD_PROJ = 3 * CONV_C

RMS_EPS = 1e-6
LN_EPS = 1e-5
RWKV_GN_EPS = 64e-5
GLA_GATE_NORM = 16.0
PROMPT_CHUNK = 64
SAMPLE_PAD_LEN = 8
PROMPT_SEQS_PER_STEP = 8
SAMPLE_SEQS_PER_STEP = 16
CONV_BLOCK = 512
CONV_UNROLL = 16
LANE = 128
SUBLANE = 8
BF16_SUBLANES = 16
HEAD_GROUP = 2
VMEM_LIMIT_BYTES = 48 * 1024 * 1024

NN = ((1,), (0,))
NT = ((1,), (1,))
TN = ((0,), (0,))


def _dg(a, b, dims=NN):
    return lax.dot_general(a, b, (dims, ((), ())), preferred_element_type=f32)


def _dot1(a, b, dims=NN):
    return _dg(a.astype(bf16), b.astype(bf16), dims)


def _split2(x):
    hi = x.astype(bf16)
    lo = (x - hi.astype(f32)).astype(bf16)
    return hi, lo


def _split3(x):
    hi = x.astype(bf16)
    r1 = x - hi.astype(f32)
    mid = r1.astype(bf16)
    lo = (r1 - mid.astype(f32)).astype(bf16)
    return hi, mid, lo


def _dot3(a, b_ref):
    ah, al = _split2(a)
    bh, bl = b_ref[0], b_ref[1]
    return _dg(ah, bh) + _dg(ah, bl) + _dg(al, bh)


def _split_weight(w):
    hi = w.astype(bf16)
    return jnp.stack([hi, (w - hi.astype(f32)).astype(bf16)])


def _dot_exact_lhs(m, x):
    hi, mid, lo = _split3(x)
    return _dg(m, hi) + _dg(m, mid) + _dg(m, lo)


def _sigmoid(x):
    return 0.5 * jnp.tanh(0.5 * x) + 0.5


def _log_sigmoid(x):
    return jnp.minimum(x, 0.0) - jnp.log(1.0 + jnp.exp(-jnp.abs(x)))


def _rmsnorm(x, g):
    return x * lax.rsqrt(jnp.mean(x * x, axis=-1, keepdims=True) + RMS_EPS) * g


def _inproj_kernel(x_ref, g_ref, w_ref, b_ref, o_ref):
    h = _rmsnorm(x_ref[...], g_ref[...])
    o_ref[...] = _dg(h.astype(bf16), w_ref[...]) + b_ref[...]


def _inproj(x2d, g, w, b):
    m, n = x2d.shape[0], w.shape[1]
    tm = 512
    return pl.pallas_call(
        _inproj_kernel,
        grid=(m // tm,),
        in_specs=[
            pl.BlockSpec((tm, D_MODEL), lambda i: (i, 0)),
            pl.BlockSpec((1, D_MODEL), lambda i: (0, 0)),
            pl.BlockSpec((D_MODEL, n), lambda i: (0, 0)),
            pl.BlockSpec((1, n), lambda i: (0, 0)),
        ],
        out_specs=pl.BlockSpec((tm, n), lambda i: (i, 0)),
        out_shape=jax.ShapeDtypeStruct((m, n), f32),
        compiler_params=pltpu.CompilerParams(
            dimension_semantics=("arbitrary",), vmem_limit_bytes=VMEM_LIMIT_BYTES),
        name="inproj",
    )(x2d, g, w, b)


def _outproj_kernel(o_ref, x_ref, w_ref, fn_ref, y_ref):
    y_ref[...] = _rmsnorm(x_ref[...] + _dg(o_ref[...].astype(bf16), w_ref[...]), fn_ref[...])


def _outproj(o2d, x2d, w, fn):
    m = x2d.shape[0]
    tm = 512
    return pl.pallas_call(
        _outproj_kernel,
        grid=(m // tm,),
        in_specs=[
            pl.BlockSpec((tm, D_MODEL), lambda i: (i, 0)),
            pl.BlockSpec((tm, D_MODEL), lambda i: (i, 0)),
            pl.BlockSpec((D_MODEL, D_MODEL), lambda i: (0, 0)),
            pl.BlockSpec((1, D_MODEL), lambda i: (0, 0)),
        ],
        out_specs=pl.BlockSpec((tm, D_MODEL), lambda i: (i, 0)),
        out_shape=jax.ShapeDtypeStruct((m, D_MODEL), f32),
        compiler_params=pltpu.CompilerParams(
            dimension_semantics=("arbitrary",), vmem_limit_bytes=VMEM_LIMIT_BYTES),
        name="outproj",
    )(o2d, x2d, w, fn)


def _midproj_kernel(o_ref, x_ref, wo_ref, g_ref, wi_ref, b_ref, xn_ref, p_ref):
    y = x_ref[...] + _dg(o_ref[...].astype(bf16), wo_ref[...])
    xn_ref[...] = y
    p_ref[...] = _dg(_rmsnorm(y, g_ref[...]).astype(bf16), wi_ref[...]) + b_ref[...]


def _midproj(o2d, x2d, w_out, g, w_in, b):
    m, n = x2d.shape[0], w_in.shape[1]
    tm = 512
    row_block = pl.BlockSpec((tm, D_MODEL), lambda i: (i, 0))
    resident = lambda shape: pl.BlockSpec(shape, lambda i: (0, 0), pipeline_mode=pl.Buffered(1))
    return pl.pallas_call(
        _midproj_kernel,
        grid=(m // tm,),
        in_specs=[
            row_block, row_block,
            resident((D_MODEL, D_MODEL)), resident((1, D_MODEL)), resident((D_MODEL, n)), resident((1, n)),
        ],
        out_specs=[row_block, pl.BlockSpec((tm, n), lambda i: (i, 0))],
        out_shape=[jax.ShapeDtypeStruct((m, D_MODEL), f32), jax.ShapeDtypeStruct((m, n), f32)],
        compiler_params=pltpu.CompilerParams(
            dimension_semantics=("arbitrary",), vmem_limit_bytes=VMEM_LIMIT_BYTES),
        name="midproj",
    )(o2d, x2d, w_out, g, w_in, b)


N_EVEN_WEIGHTS = 16


def _even_mix_kernel(*refs, t, n_valid, nb, has_state, n_alias):
    p_ref = refs[0]
    n_in = 1 + (3 if has_state else 0)
    (mu_ref, w0_ref, wup_ref, a0_ref, aup_ref, kk_ref, ka_ref, rk_ref, lnw_ref, lnb_ref, gup_ref, gb_ref, gn_ref,
     bd_ref, fmask_ref, tmask_ref) = refs[n_in:n_in + N_EVEN_WEIGHTS]
    og_ref, sr_ref, sg_ref, prev_ref, srs_ref = refs[n_in + N_EVEN_WEIGHTS + n_alias:]
    c = pl.program_id(1)
    n_groups = RWKV_HEADS // HEAD_GROUP

    @pl.when(c == 0)
    def _():
        if not has_state:
            sg_ref[...] = jnp.zeros(sg_ref.shape, f32)
            prev_ref[...] = jnp.zeros(prev_ref.shape, f32)
            srs_ref[...] = jnp.zeros(srs_ref.shape, f32)
            return
        shift_ref, sr_in_ref, sg_in_ref = refs[1:4]
        sg_ref[...] = sg_in_ref[...]
        prev_ref[...] = shift_ref[...]
        for i in range(nb):
            for q in range(n_groups):
                rows_ = []
                for j in range(HEAD_GROUP):
                    parts = [jnp.zeros((RWKV_HEAD, RWKV_HEAD), f32)] * HEAD_GROUP
                    parts[j] = sr_in_ref[i, q * HEAD_GROUP + j]
                    rows_.append(jnp.concatenate(parts, axis=1))
                srs_ref[i, q] = jnp.concatenate(rows_, axis=0)

    row = lax.broadcasted_iota(jnp.int32, (t, t), 0)
    col = lax.broadcasted_iota(jnp.int32, (t, t), 1)
    incl = row >= col
    tril = jnp.where(incl, 1.0, 0.0).astype(bf16)
    rowv = lax.broadcasted_iota(jnp.int32, (t, 1), 0)
    valid = None if n_valid == t else rowv < n_valid
    bd = bd_ref[...]
    rk64 = lax.broadcasted_iota(jnp.int32, (GLA_DK, GLA_DK), 0)
    ck64 = lax.broadcasted_iota(jnp.int32, (GLA_DK, GLA_DK), 1)

    def head_sum(x):
        tiles = x.shape[1] // LANE
        stacked = jnp.concatenate([x[:, m * LANE:(m + 1) * LANE] for m in range(tiles)], axis=0)
        hi, lo = _split2(stacked)
        s = _dg(hi, bd) + _dg(lo, bd)
        return jnp.concatenate([s[m * t:(m + 1) * t] for m in range(tiles)], axis=1)

    gpairs = [(i, h) for i in range(nb) for h in range(GLA_HEADS)]

    seq = []
    for i in range(nb):
        pr = p_ref[i, :, 0:RWKV_PROJ]
        p_prev = jnp.where(rowv == 0, prev_ref[i], pltpu.roll(pr, 1, 0))
        prev_ref[i] = pr[t - 1:t, :]
        xm = pr + (p_prev - pr) * mu_ref[...]
        r = xm[:, 0:RWKV_W]
        k = xm[:, RWKV_W:2 * RWKV_W]
        v = xm[:, 2 * RWKV_W:3 * RWKV_W]
        xwa = xm[:, 3 * RWKV_W:RWKV_PROJ]
        wl = w0_ref[...] + _dot3(jnp.tanh(xwa), wup_ref)
        al = a0_ref[...] + _dot3(xwa, aup_ref)
        ld = (-math.exp(-0.5)) * _sigmoid(wl)
        a = _sigmoid(al)
        kkr = k * kk_ref[...]
        kkn = kkr * lax.rsqrt(jnp.maximum(head_sum(kkr * kkr), 1e-24))
        k2 = k * (1.0 + (a - 1.0) * ka_ref[...])
        if valid is not None:
            ld = jnp.where(valid, ld, 0.0)
            kkn = jnp.where(valid, kkn, 0.0)
            k2 = jnp.where(valid, k2, 0.0)
        c_in = _dot_exact_lhs(tril, ld)
        g_in = jnp.exp(c_in)
        ginv = jnp.exp(-c_in)
        seq.append(dict(r=r, k2=k2, v=v, g_last=g_in[t - 1:t, :],
                        abar=-kkn * jnp.exp(c_in - ld), bbar=kkn * a * ginv, kbar=k2 * ginv, rbar=r * g_in))

    fw = HEAD_GROUP * RWKV_HEAD
    tw = HEAD_GROUP * t
    rt = lax.broadcasted_iota(jnp.int32, (t, tw), 0)
    st = lax.broadcasted_iota(jnp.int32, (t, tw), 1) % t
    strict_g = rt > st
    incl_g = rt >= st
    eye_g = jnp.where(rt == st, 1.0, 0.0).astype(f32)
    first_level_g = (rt // 2 == st // 2) & (rt % 2 == 1) & (st % 2 == 0)
    rs = lax.broadcasted_iota(jnp.int32, (fw, fw), 0) // RWKV_HEAD
    cs = lax.broadcasted_iota(jnp.int32, (fw, fw), 1) // RWKV_HEAD
    state_diag = rs == cs

    def block_diag(x, mask):
        if x.shape[0] % BF16_SUBLANES == 0:
            xb = x.astype(bf16)
            return jnp.concatenate([xb] * HEAD_GROUP, axis=0) * mask
        return (jnp.concatenate([x] * HEAD_GROUP, axis=0) * mask.astype(f32)).astype(bf16)

    fmask = fmask_ref[...]
    groups = [(i, q) for i in range(nb) for q in range(RWKV_HEADS // HEAD_GROUP)]
    fs = lambda q: slice(q * fw, (q + 1) * fw)
    lm = {(i, q): jnp.concatenate([seq[i]["abar"][:, fs(q)], seq[i]["rbar"][:, fs(q)]], axis=0).astype(bf16)
          for i, q in groups}
    vq = {(i, q): seq[i]["v"][:, fs(q)] for i, q in groups}
    vbd = {iq: block_diag(vq[iq], fmask) for iq in groups}
    gb = {(i, q): _dg(lm[i, q], block_diag(seq[i]["bbar"][:, fs(q)], fmask), NT) for i, q in groups}
    gk = {(i, q): _dg(lm[i, q], block_diag(seq[i]["kbar"][:, fs(q)], fmask), NT) for i, q in groups}
    a_ab = {iq: jnp.where(strict_g, gb[iq][:t], 0.0) for iq in groups}
    a_ak = {iq: jnp.where(strict_g, gk[iq][:t], 0.0).astype(bf16) for iq in groups}
    a_rb = {iq: jnp.where(incl_g, gb[iq][t:], 0.0).astype(bf16) for iq in groups}
    a_rk = {iq: jnp.where(incl_g, gk[iq][t:], 0.0).astype(bf16) for iq in groups}
    minv = {iq: eye_g + jnp.where(first_level_g, a_ab[iq], 0.0) for iq in groups}
    for lvl in range(1, tmask_ref.shape[0]):
        tmp = {iq: _dg(minv[iq].astype(bf16), block_diag(a_ab[iq], tmask_ref[lvl])) for iq in groups}
        minv = {iq: minv[iq] + _dg(tmp[iq].astype(bf16), block_diag(minv[iq], tmask_ref[0])) for iq in groups}
    akv = {iq: _dg(a_ak[iq], vbd[iq]) for iq in groups}
    s_old = {(i, q): srs_ref[i, q] for i, q in groups}
    ls = {iq: _dg(lm[iq], s_old[iq].astype(bf16), NT) for iq in groups}
    z = {iq: _dg(minv[iq].astype(bf16), block_diag(ls[iq][:t] + akv[iq], fmask)) for iq in groups}
    yq = {iq: ls[iq][t:] + _dg(a_rb[iq], block_diag(z[iq], fmask)) + _dg(a_rk[iq], vbd[iq]) for iq in groups}
    for i, q in groups:
        zv = jnp.concatenate([z[i, q], vq[i, q]], axis=0).astype(bf16)
        rm = jnp.concatenate([seq[i]["bbar"][:, fs(q)], seq[i]["kbar"][:, fs(q)]], axis=0).astype(bf16)
        upd = jnp.where(state_diag, _dg(zv, rm, TN), 0.0)
        srs_ref[i, q] = (s_old[i, q] + upd) * seq[i]["g_last"][:, fs(q)]

    gseq = []
    for i in range(nb):
        q = p_ref[i, :, Q_OFF:K_OFF]
        kg = p_ref[i, :, K_OFF:V_OFF]
        gd = p_ref[i, :, GD_OFF:GATE_OFF]
        la = _log_sigmoid(_dot3(gd, gup_ref) + gb_ref[...]) * (1.0 / GLA_GATE_NORM)
        if valid is not None:
            la = jnp.where(valid, la, 0.0)
            kg = jnp.where(valid, kg, 0.0)
        b = _dot_exact_lhs(tril, la)
        b_last = b[t - 1:t, :]
        gseq.append(dict(qg=(q * jnp.exp(b) * (GLA_DK ** -0.5)).astype(bf16), kgi=(kg * jnp.exp(-b)).astype(bf16),
                         kd=(kg * jnp.exp(b_last - b)).astype(bf16), eb=jnp.exp(b_last)))
    gs = lambda h: slice(h * GLA_DK, (h + 1) * GLA_DK)
    gv = {(i, h): p_ref[i, :, V_OFF + h * GLA_DV:V_OFF + (h + 1) * GLA_DV].astype(bf16) for i, h in gpairs}
    att = {(i, h): jnp.where(incl, _dg(gseq[i]["qg"][:, gs(h)], gseq[i]["kgi"][:, gs(h)], NT), 0.0).astype(bf16)
           for i, h in gpairs}
    gs_old = {(i, h): sg_ref[i, h] for i, h in gpairs}
    go = {(i, h): _dg(att[i, h], gv[i, h]) + _dg(gseq[i]["qg"][:, gs(h)], gs_old[i, h].astype(bf16))
          for i, h in gpairs}
    for i, h in gpairs:
        e_col = jnp.sum(jnp.where(rk64 == ck64, gseq[i]["eb"][:, gs(h)], 0.0), axis=1, keepdims=True)
        sg_ref[i, h] = e_col * gs_old[i, h] + _dg(gseq[i]["kd"][:, gs(h)], gv[i, h], TN)

    inv_k = 1.0 / RWKV_HEAD
    for i in range(nb):
        y = jnp.concatenate([yq[i, q] for q in range(n_groups)], axis=1)
        mean = head_sum(y) * inv_k
        dlt = y - mean
        var = head_sum(dlt * dlt) * inv_k
        yn = dlt * lax.rsqrt(var + RWKV_GN_EPS) * lnw_ref[...] + lnb_ref[...]
        o_r = yn + head_sum(seq[i]["r"] * seq[i]["k2"] * rk_ref[...]) * seq[i]["v"]
        o_g = jnp.concatenate([_rmsnorm(go[i, h], gn_ref[...]) for h in range(GLA_HEADS)], axis=1)
        gate = p_ref[i, :, GATE_OFF:EVEN_PROJ_PAD]
        og_ref[i] = jnp.concatenate([o_r, o_g], axis=1) * (gate * _sigmoid(gate))

    @pl.when(c == pl.num_programs(1) - 1)
    def _():
        for i in range(nb):
            for q in range(n_groups):
                for j in range(HEAD_GROUP):
                    lo = j * RWKV_HEAD
                    sr_ref[i, q * HEAD_GROUP + j] = srs_ref[i, q, lo:lo + RWKV_HEAD, lo:lo + RWKV_HEAD]


def _group_masks(t):
    rows = np.arange(HEAD_GROUP * t)[:, None]
    fmask = rows // t == np.arange(HEAD_GROUP * RWKV_HEAD)[None, :] // RWKV_HEAD
    lanes = np.arange(HEAD_GROUP * t)[None, :]
    same_head = rows // t == lanes // t
    r, s = rows % t, lanes % t
    levels = [same_head]
    blk = 2
    while blk < t:
        levels.append(same_head & (r // (2 * blk) == s // (2 * blk)) & ((r // blk) % 2 == 1) & ((s // blk) % 2 == 0))
        blk *= 2
    return jnp.asarray(fmask, bf16), jnp.asarray(np.stack(levels), bf16)


def _even_mix(p, layer, n_layers, states, prev_outs, wts, t, n_valid, nb):
    bsz, lp, _ = p.shape
    nc = lp // t
    const = lambda shape: pl.BlockSpec(shape, lambda b, c: (0,) * len(shape))
    fmask, tmask = _group_masks(t)
    n_groups = RWKV_HEADS // HEAD_GROUP
    slab = HEAD_GROUP * RWKV_HEAD
    sr_block = (None, nb, RWKV_HEADS, RWKV_HEAD, RWKV_HEAD)
    sg_block = (None, nb, GLA_HEADS, GLA_DK, GLA_DV)
    layer_map = lambda b, c: (layer, b, 0, 0, 0)
    state_specs, state_args = [], []
    if states is not None:
        shift, sr0, sg0 = states
        state_specs = [pl.BlockSpec((None, nb, 1, RWKV_PROJ), lambda b, c: (layer, b, 0, 0)),
                       pl.BlockSpec(sr_block, layer_map), pl.BlockSpec(sg_block, layer_map)]
        state_args = [shift.reshape(shift.shape[0], bsz, 1, RWKV_PROJ), sr0, sg0]
    alias_specs, alias_args, aliases = [], [], {}
    if prev_outs is not None:
        alias_specs = [pl.BlockSpec(memory_space=pl.ANY)] * 2
        alias_args = list(prev_outs)
        first = 1 + len(state_args) + N_EVEN_WEIGHTS
        aliases = {first: 1, first + 1: 2}
    return pl.pallas_call(
        functools.partial(_even_mix_kernel, t=t, n_valid=n_valid, nb=nb, has_state=states is not None,
                          n_alias=len(alias_args)),
        grid=(bsz // nb, nc),
        in_specs=[
            pl.BlockSpec((nb, t, EVEN_PROJ_PAD), lambda b, c: (b, c, 0)),
            *state_specs,
            const((1, RWKV_PROJ)),
            const((1, RWKV_W)), const((2, 2 * RWKV_LORA, RWKV_W)),
            const((1, RWKV_W)), const((2, 2 * RWKV_LORA, RWKV_W)),
            const((1, RWKV_W)), const((1, RWKV_W)), const((1, RWKV_W)),
            const((1, RWKV_W)), const((1, RWKV_W)),
            const((2, GLA_LORA_PAD, GLA_KW)), const((1, GLA_KW)), const((1, GLA_DV)),
            const((LANE, LANE)), const(fmask.shape), const(tmask.shape),
            *alias_specs,
        ],
        out_specs=[
            pl.BlockSpec((nb, t, EVEN_WIDTH), lambda b, c: (b, c, 0)),
            pl.BlockSpec(sr_block, layer_map),
            pl.BlockSpec(sg_block, layer_map),
        ],
        out_shape=[
            jax.ShapeDtypeStruct((bsz, lp, EVEN_WIDTH), f32),
            jax.ShapeDtypeStruct((n_layers, bsz, RWKV_HEADS, RWKV_HEAD, RWKV_HEAD), f32),
            jax.ShapeDtypeStruct((n_layers, bsz, GLA_HEADS, GLA_DK, GLA_DV), f32),
        ],
        input_output_aliases=aliases,
        scratch_shapes=[pltpu.VMEM((nb, 1, RWKV_PROJ), f32), pltpu.VMEM((nb, n_groups, slab, slab), f32)],
        compiler_params=pltpu.CompilerParams(
            dimension_semantics=("arbitrary", "arbitrary"), vmem_limit_bytes=VMEM_LIMIT_BYTES),
        name="even_mix",
    )(p, *state_args, *wts, fmask, tmask, *alias_args)


def _odd_mix_kernel(*refs, tb, n_valid, nb, has_cache, n_alias, pos_major):
    p_ref = refs[0]
    n_in = 2 if has_cache else 1
    cw_ref, cb_ref, lnw_ref, lnb_ref = refs[n_in:n_in + 4]
    o_ref, cache_out_ref, bases_ref, ext_ref, y_ref = refs[n_in + 4 + n_alias:]
    seq_view = (lambda ref, i: ref.at[:, i]) if pos_major else (lambda ref, i: ref.at[i])
    for i in range(nb):
        _odd_mix_one(p_ref.at[i], seq_view(refs[1], i) if has_cache else None, cw_ref, cb_ref, lnw_ref, lnb_ref,
                     o_ref.at[i], seq_view(cache_out_ref, i), bases_ref.at[i], ext_ref, y_ref,
                     tb=tb, n_valid=n_valid)


def _odd_mix_one(p_ref, cache_ref, cw_ref, cb_ref, lnw_ref, lnb_ref, o_ref, cache_out_ref, base_ref, ext_ref,
                 y_ref, *, tb, n_valid):
    c = pl.program_id(1)
    pad = CONV_HIST_PAD - CONV_HIST
    rows = CONV_HIST_PAD + tb

    @pl.when(c == 0)
    def _():
        if cache_ref is not None:
            base_ref[0:SUBLANE, :] = jnp.zeros((SUBLANE, CONV_C), f32)
            base_ref[pad:CONV_HIST_PAD, :] = cache_ref[...]
        else:
            base_ref[0:CONV_HIST_PAD, :] = jnp.zeros((CONV_HIST_PAD, CONV_C), f32)
        base_ref[rows:rows + SUBLANE, :] = jnp.zeros((SUBLANE, CONV_C), f32)

    base_ref[CONV_HIST_PAD:rows, :] = p_ref[:, 0:CONV_C] * _sigmoid(p_ref[:, CONV_C:2 * CONV_C])
    for sh in range(SUBLANE):
        for ct in range(CONV_C // LANE):
            ext_ref[sh, ct] = base_ref[sh:sh + rows, ct * LANE:(ct + 1) * LANE]

    taps = [[j for j in range(CONV_W) if (pad + j) % SUBLANE == sh] for sh in range(SUBLANE)]
    first = [((pad + taps[sh][0]) - sh) // SUBLANE for sh in range(SUBLANE)]
    depth = [((pad + taps[sh][-1]) - sh) // SUBLANE - first[sh] + 1 for sh in range(SUBLANE)]
    for ct in range(CONV_C // LANE):
        ls = slice(ct * LANE, (ct + 1) * LANE)
        bias = jnp.broadcast_to(cb_ref[:, ls], (SUBLANE, LANE))

        def tile(sh, idx, ct=ct):
            return ext_ref[sh, ct, pl.ds(pl.multiple_of(idx * SUBLANE, SUBLANE), SUBLANE), :]

        def body(k, carry, ls=ls, bias=bias, tile=tile):
            new_carry, partial = [], []
            for sh in range(SUBLANE):
                tiles = list(carry[sh]) + [tile(sh, k + first[sh] + depth[sh] - 1)]
                acc = None
                for j in taps[sh]:
                    term = cw_ref[j:j + 1, ls] * tiles[((pad + j) - sh) // SUBLANE - first[sh]]
                    acc = term if acc is None else acc + term
                partial.append(acc)
                new_carry.append(tuple(tiles[1:]))
            while len(partial) > 1:
                partial = [partial[i] + partial[i + 1] for i in range(0, len(partial), 2)]
            y_ref[pl.ds(pl.multiple_of(k * SUBLANE, SUBLANE), SUBLANE), ls] = partial[0] + bias
            return tuple(new_carry)

        init = tuple(tuple(tile(sh, first[sh] + a) for a in range(depth[sh] - 1)) for sh in range(SUBLANE))
        lax.fori_loop(0, tb // SUBLANE, body, init, unroll=min(CONV_UNROLL, tb // SUBLANE))

    y = y_ref[...]
    mu = jnp.mean(y, axis=-1, keepdims=True)
    d = y - mu
    var = jnp.mean(d * d, axis=-1, keepdims=True)
    yn = d * lax.rsqrt(var + LN_EPS) * lnw_ref[...] + lnb_ref[...]
    g = p_ref[:, 2 * CONV_C:3 * CONV_C]
    o_ref[...] = (yn * _sigmoid(yn)) * (g * _sigmoid(g))

    @pl.when(c == pl.num_programs(1) - 1)
    def _():
        cache_out_ref[...] = base_ref[pad + n_valid:pad + n_valid + CONV_HIST, :]

    base_ref[0:CONV_HIST_PAD, :] = base_ref[tb:rows, :]


def _odd_mix(p, layer, n_layers, cache, prev_out, cw, cb, lnw, lnb, tb, n_valid, nb, pos_major):
    bsz, lp, _ = p.shape
    nc = lp // tb
    const = lambda shape: pl.BlockSpec(shape, lambda b, c: (0,) * len(shape))
    if pos_major:
        cache_block, cache_shape = (None, CONV_HIST, nb, CONV_C), (n_layers, CONV_HIST, bsz, CONV_C)
        layer_map = lambda b, c: (layer, 0, b, 0)
    else:
        cache_block, cache_shape = (None, nb, CONV_HIST, CONV_C), (n_layers, bsz, CONV_HIST, CONV_C)
        layer_map = lambda b, c: (layer, b, 0, 0)
    cache_specs = [] if cache is None else [pl.BlockSpec(cache_block, layer_map)]
    cache_args = [] if cache is None else [cache]
    alias_specs = [] if prev_out is None else [pl.BlockSpec(memory_space=pl.ANY)]
    alias_args = [] if prev_out is None else [prev_out]
    aliases = {} if prev_out is None else {1 + len(cache_args) + 4: 1}
    return pl.pallas_call(
        functools.partial(_odd_mix_kernel, tb=tb, n_valid=n_valid, nb=nb, has_cache=cache is not None,
                          n_alias=len(alias_args), pos_major=pos_major),
        grid=(bsz // nb, nc),
        in_specs=[
            pl.BlockSpec((nb, tb, ODD_PROJ), lambda b, c: (b, c, 0)),
            *cache_specs,
            const((CONV_W, CONV_C)), const((1, CONV_C)), const((1, CONV_C)), const((1, CONV_C)),
            *alias_specs,
        ],
        out_specs=[
            pl.BlockSpec((nb, tb, CONV_C), lambda b, c: (b, c, 0)),
            pl.BlockSpec(cache_block, layer_map),
        ],
        out_shape=[
            jax.ShapeDtypeStruct((bsz, lp, CONV_C), f32),
            jax.ShapeDtypeStruct(cache_shape, f32),
        ],
        input_output_aliases=aliases,
        scratch_shapes=[pltpu.VMEM((nb, CONV_HIST_PAD + tb + SUBLANE, CONV_C), f32),
                        pltpu.VMEM((SUBLANE, CONV_C // LANE, CONV_HIST_PAD + tb, LANE), f32),
                        pltpu.VMEM((tb, CONV_C), f32)],
        compiler_params=pltpu.CompilerParams(
            dimension_semantics=("arbitrary", "arbitrary"), vmem_limit_bytes=VMEM_LIMIT_BYTES),
        name="odd_mix",
    )(p, *cache_args, cw, cb, lnw, lnb, *alias_args)


def _row(x):
    return x.reshape(1, -1)


def kernel(x_prompt, x_sample, state_rwkv_shift, state_rwkv, state_gla, cache_conv, norm_even, w_in_even, rwkv_mu, rwkv_w0, rwkv_w_up, rwkv_a0, rwkv_a_up, rwkv_k_k, rwkv_k_a, rwkv_r_k, rwkv_ln_w, rwkv_ln_b, gla_g_up, gla_g_b, gla_norm, w_out_even, norm_odd, w_in_odd, b_in_odd, conv_w, conv_b, conv_ln_w, conv_ln_b, w_out_odd, final_norm):
    bp, lp, _ = x_prompt.shape
    bs, ls, _ = x_sample.shape
    depth = norm_even.shape[0] + norm_odd.shape[0]
    xs_pad = jnp.pad(x_sample, ((0, 0), (0, SAMPLE_PAD_LEN - ls), (0, 0)))
    groups = [
        dict(x=x_prompt.reshape(bp * lp, D_MODEL), b=bp, l=lp, nv=lp, t=PROMPT_CHUNK, tb=CONV_BLOCK, prompt=True,
             nb=PROMPT_SEQS_PER_STEP, cnb=1),
        dict(x=xs_pad.reshape(bs * SAMPLE_PAD_LEN, D_MODEL), b=bs, l=SAMPLE_PAD_LEN, nv=ls, t=SAMPLE_PAD_LEN,
             tb=SAMPLE_PAD_LEN, prompt=False, nb=SAMPLE_SEQS_PER_STEP, cnb=SAMPLE_SEQS_PER_STEP),
    ]
    bd = (jnp.arange(LANE)[:, None] // RWKV_HEAD == jnp.arange(LANE)[None, :] // RWKV_HEAD).astype(bf16)
    zeros_lora = jnp.zeros((RWKV_LORA, RWKV_W), f32)
    outs = [dict(shift=[], states=None, conv=None) for _ in groups]
    n_even, n_odd = norm_even.shape[0], norm_odd.shape[0]

    proj = []
    for layer in range(depth):
        i = layer // 2
        if layer % 2 == 0:
            w = w_in_even[i]
            w_pad = jnp.concatenate(
                [w[:, :RWKV_PROJ + GLA_PROJ], jnp.zeros((D_MODEL, GLA_LORA_PAD - GLA_LORA), f32),
                 w[:, RWKV_PROJ + GLA_PROJ:]], axis=1).astype(bf16)
            proj.append((_row(norm_even[i]), w_pad, jnp.zeros((1, EVEN_PROJ_PAD), f32), w_out_even[i].astype(bf16)))
        else:
            proj.append((_row(norm_odd[i]), w_in_odd[i].astype(bf16), _row(b_in_odd[i]), w_out_odd[i].astype(bf16)))
    for g in groups:
        g["p"] = _inproj(g["x"], *proj[0][:3])

    def next_layer(g, layer, o2d):
        if layer == depth - 1:
            g["x"] = _outproj(o2d, g["x"], proj[layer][3], _row(final_norm))
        else:
            g["x"], g["p"] = _midproj(o2d, g["x"], proj[layer][3], *proj[layer + 1][:3])

    for layer in range(depth):
        i = layer // 2
        if layer % 2 == 0:
            wts = (
                _row(rwkv_mu[i]), _row(rwkv_w0[i]),
                _split_weight(jnp.concatenate([rwkv_w_up[i], zeros_lora], axis=0)),
                _row(rwkv_a0[i]), _split_weight(jnp.concatenate([zeros_lora, rwkv_a_up[i]], axis=0)),
                _row(rwkv_k_k[i]), _row(rwkv_k_a[i]), _row(rwkv_r_k[i]), _row(rwkv_ln_w[i]), _row(rwkv_ln_b[i]),
                _split_weight(jnp.concatenate([gla_g_up[i], jnp.zeros((GLA_LORA_PAD - GLA_LORA, GLA_KW), f32)],
                                              axis=0)),
                _row(gla_g_b[i]), _row(gla_norm[i]), bd,
            )
            for gi, g in enumerate(groups):
                p = g["p"].reshape(g["b"], g["l"], EVEN_PROJ_PAD)
                states = None if g["prompt"] else (state_rwkv_shift, state_rwkv, state_gla)
                og, sr, sg = _even_mix(p, i, n_even, states, outs[gi]["states"], wts, g["t"],
                                       min(g["nv"], g["t"]), g["nb"])
                outs[gi]["shift"].append(p[:, g["nv"] - 1, :RWKV_PROJ])
                outs[gi]["states"] = (sr, sg)
                next_layer(g, layer, og.reshape(-1, EVEN_WIDTH))
        else:
            for gi, g in enumerate(groups):
                p = g["p"].reshape(g["b"], g["l"], ODD_PROJ)
                cache = None if g["prompt"] else jnp.transpose(cache_conv, (0, 2, 1, 3))
                y, outs[gi]["conv"] = _odd_mix(p, i, n_odd, cache, outs[gi]["conv"], conv_w[i], _row(conv_b[i]),
                                               _row(conv_ln_w[i]), _row(conv_ln_b[i]), g["tb"],
                                               min(g["nv"], g["tb"]), g["cnb"], not g["prompt"])
                next_layer(g, layer, y.reshape(-1, CONV_C))

    y_prompt = groups[0]["x"].reshape(bp, lp, D_MODEL)
    y_sample = groups[1]["x"].reshape(bs, SAMPLE_PAD_LEN, D_MODEL)[:, :ls]
    op, os_ = outs
    return (y_prompt, y_sample,
            jnp.stack(op["shift"]), jnp.stack(os_["shift"]),
            op["states"][0], os_["states"][0],
            op["states"][1], os_["states"][1],
            op["conv"], jnp.transpose(os_["conv"], (0, 2, 1, 3)))
```
